```python
import jax
import jax.numpy as jnp
from jax import lax
import numpy as np

D_MODEL = 2048
BATCH = 8
SEQ = 4096
DEPTH = 4
DEC_BATCH = 8
DEC_SEQ = 16
PAST_LEN = 4096

CHUNK = 64
NORM_EPS = 1e-6
L2_EPS = 1e-6
GLA_HEADS = 4
GLA_DK = 64
GLA_DV = 128
GLA_LORA = 16
GLA_GATE_TEMP = 16.0
GDN_HEADS = 6
GDN_DK = 128
GDN_DV = 128
CONV_W = 4
RW_HEADS = 12
RW_N = 64
RW_DECAY_LORA = 64
RW_AAA_LORA = 64
RW_MV_LORA = 32
RW_GATE_LORA = 128
RW_GN_EPS = 64e-5

GLA_QK = GLA_HEADS * GLA_DK
GLA_V = GLA_HEADS * GLA_DV
GDN_QK = GDN_HEADS * GDN_DK
GDN_V = GDN_HEADS * GDN_DV
GDN_CONV_CH = 2 * GDN_QK + GDN_V
RW_C = RW_HEADS * RW_N
MIX_WIDTH = GLA_V + GDN_V + RW_C
GLA_SIZES = (GLA_QK, GLA_QK, GLA_V, GLA_LORA, GLA_V)
GDN_SIZES = (GDN_CONV_CH, GDN_HEADS, GDN_HEADS, GDN_V)
RW_SIZES = (RW_C, RW_C, RW_C, RW_DECAY_LORA, RW_AAA_LORA, RW_GATE_LORA)
GLA_PROJ = sum(GLA_SIZES)
GDN_PROJ = sum(GDN_SIZES)
RW_PROJ = sum(RW_SIZES)
PROJ_WIDTH = GLA_PROJ + GDN_PROJ + RW_PROJ
D_FF = -(-8 * D_MODEL // (3 * 256)) * 256

kernel_name = 'hybrid_gla_gdn_rwkv7_stream_step'


def _split(z, sizes):
    idx = [int(i) for i in np.cumsum(sizes)[:-1]]
    return jnp.split(z, idx, axis=-1)


def rmsnorm(x, g):
    x32 = x.astype(jnp.float32)
    y = x32 * lax.rsqrt(jnp.mean(x32 * x32, axis=-1, keepdims=True) + NORM_EPS)
    return (y * g.astype(jnp.float32)).astype(x.dtype)


def l2norm(x):
    return x * lax.rsqrt(jnp.sum(x * x, axis=-1, keepdims=True) + L2_EPS)


def causal_conv(x, buf, w):
    xc = jnp.concatenate([buf.astype(x.dtype), x], axis=1)
    t = x.shape[1]
    y = xc[:, 0:t] * w[0]
    for j in range(1, CONV_W):
        y = y + xc[:, j:j + t] * w[j]
    return y, xc[:, t:]


def _to_blocks(a, c):
    b, t = a.shape[0], a.shape[1]
    a = a.reshape((b, t // c, c) + a.shape[2:])
    return jnp.moveaxis(a, (1, 3), (0, 2))


def _from_blocks(o):
    o = jnp.moveaxis(o, (0, 2), (1, 3))
    return o.reshape((o.shape[0], o.shape[1] * o.shape[2]) + o.shape[3:])


def gla_recurrence(q, k, v, loga, s0):
    c = min(CHUNK, q.shape[1])
    causal = jnp.tril(jnp.ones((c, c), bool))

    def step(s, inp):
        qi, ki, vi, li = inp
        b = jnp.cumsum(li, axis=2)
        o = jnp.einsum('bhtd,bhde->bhte', qi * jnp.exp(b), s)
        diff = b[:, :, :, None, :] - b[:, :, None, :, :]
        dec = jnp.exp(jnp.where(causal[:, :, None], diff, -jnp.inf))
        scores = jnp.einsum('bhtd,bhsd,bhtsd->bhts', qi, ki, dec)
        o = o + jnp.einsum('bhts,bhse->bhte', scores, vi)
        b_last = b[:, :, -1:, :]
        s_new = jnp.exp(b[:, :, -1, :])[..., None] * s + jnp.einsum(
            'bhsd,bhse->bhde', ki * jnp.exp(b_last - b), vi)
        return s_new, o

    xs = tuple(_to_blocks(a, c) for a in (q, k, v, loga))
    s, o = lax.scan(step, s0, xs)
    return _from_blocks(o), s


def gdn_recurrence(q, k, v, beta, loga, s0):
    c = min(CHUNK, q.shape[1])
    causal = jnp.tril(jnp.ones((c, c), bool))
    strict = jnp.tril(jnp.ones((c, c), bool), -1)
    eye = jnp.eye(c, dtype=jnp.float32)

    def step(s, inp):
        qi, ki, vi, bi, li = inp
        b = jnp.cumsum(li, axis=-1)
        dec = jnp.exp(jnp.where(causal, b[..., :, None] - b[..., None, :], -jnp.inf))
        kk = jnp.einsum('bhtd,bhsd->bhts', ki, ki)
        lower = jnp.where(strict, bi[..., :, None] * dec * kk, 0.0)
        eb = jnp.exp(b)[..., None]
        rhs = bi[..., None] * (vi - eb * jnp.einsum('bhtd,bhde->bhte', ki, s))
        delta = lax.linalg.triangular_solve(lower + eye, rhs, left_side=True,
                                            lower=True, unit_diagonal=True)
        qk = jnp.einsum('bhtd,bhsd->bhts', qi, ki) * dec
        o = eb * jnp.einsum('bhtd,bhde->bhte', qi, s) + jnp.einsum('bhts,bhse->bhte', qk, delta)
        b_last = b[..., -1:]
        s_new = jnp.exp(b_last)[..., None] * s + jnp.einsum(
            'bhsd,bhse->bhde', ki * jnp.exp(b_last - b)[..., None], delta)
        return s_new, o

    xs = tuple(_to_blocks(a, c) for a in (q, k, v, beta, loga))
    s, o = lax.scan(step, s0, xs)
    return _from_blocks(o), s


def rwkv_recurrence(r, w, k, v, kk, a, s0):
    def step(s, inp):
        rt, wt, kt, vt, kkt, at = inp
        sa = jnp.einsum('bhk,bhkv->bhv', kkt, s)
        s = wt[..., None] * s - (kkt * at)[..., None] * sa[..., None, :] + kt[..., None] * vt[..., None, :]
        return s, jnp.einsum('bhk,bhkv->bhv', rt, s)

    xs = tuple(jnp.swapaxes(u, 0, 1) for u in (r, w, k, v, kk, a))
    s, o = lax.scan(step, s0, xs)
    return jnp.swapaxes(o, 0, 1), s


def mixer_layer(h, l, v_first, s_gla, s_gdn, c_gdn, s_rw, c_rw, prm):
    f32 = jnp.float32
    b, t, _ = h.shape
    P = lambda name: prm[name][l].astype(f32)
    z = jnp.matmul(h, prm['w_in'][l]).astype(f32)
    z_gla, z_gdn, z_rw = _split(z, (GLA_PROJ, GDN_PROJ, RW_PROJ))

    gq, gk, gv, ga, gg = _split(z_gla, GLA_SIZES)
    q = gq.reshape(b, t, GLA_HEADS, GLA_DK) * (GLA_DK ** -0.5)
    k = gk.reshape(b, t, GLA_HEADS, GLA_DK)
    v = gv.reshape(b, t, GLA_HEADS, GLA_DV)
    loga = jax.nn.log_sigmoid(ga @ P('gla_a_up') + P('gla_a_bias')) / GLA_GATE_TEMP
    o, s_gla_new = gla_recurrence(q, k, v, loga.reshape(b, t, GLA_HEADS, GLA_DK), s_gla.astype(f32))
    o_gla = (rmsnorm(o, P('gla_norm_g')) * jax.nn.silu(gg.reshape(b, t, GLA_HEADS, GLA_DV))).reshape(b, t, GLA_V)

    dqkv, dbeta, da, dg = _split(z_gdn, GDN_SIZES)
    conv, c_gdn_new = causal_conv(dqkv, c_gdn, P('gdn_conv_w'))
    cq, ck, cv = _split(jax.nn.silu(conv), (GDN_QK, GDN_QK, GDN_V))
    q = l2norm(cq.reshape(b, t, GDN_HEADS, GDN_DK)) * (GDN_DK ** -0.5)
    k = l2norm(ck.reshape(b, t, GDN_HEADS, GDN_DK))
    v = cv.reshape(b, t, GDN_HEADS, GDN_DV)
    beta = jax.nn.sigmoid(dbeta)
    loga = -jnp.exp(P('gdn_A_log')) * jax.nn.softplus(da + P('gdn_dt_bias'))
    o, s_gdn_new = gdn_recurrence(q, k, v, beta, loga, s_gdn.astype(f32))
    o_gdn = (rmsnorm(o, P('gdn_norm_g')) * jax.nn.silu(dg.reshape(b, t, GDN_HEADS, GDN_DV))).reshape(b, t, GDN_V)

    z_prev = jnp.concatenate([c_rw.astype(f32), z_rw[:, :-1]], axis=1)
    zm = z_rw + (z_prev - z_rw) * P('rw_mu')
    c_rw_new = z_rw[:, t - 1:]
    xr, xk, xv, xw, xa, xg = _split(zm, RW_SIZES)
    w_log = -jax.nn.softplus(-(P('rw_w0') + jnp.tanh(xw) @ P('rw_w_up'))) - 0.5
    decay = jnp.exp(-jnp.exp(w_log))
    a = jax.nn.sigmoid(P('rw_a0') + xa @ P('rw_a_up'))
    if l == 0:
        v_first = xv
    else:
        nu = jax.nn.sigmoid(prm['rw_v0'][l - 1].astype(f32)
                            + (xv @ prm['rw_v_down'][l - 1].astype(f32)) @ prm['rw_v_up'][l - 1].astype(f32))
        xv = xv + (v_first - xv) * nu
    hs = lambda u: u.reshape(b, t, RW_HEADS, RW_N)
    kk = l2norm(hs(xk * P('rw_k_k')))
    xk = xk * (1.0 + (a - 1.0) * P('rw_k_a'))
    r4, k4, v4 = hs(xr), hs(xk), hs(xv)
    o, s_rw_new = rwkv_recurrence(r4, hs(decay), k4, v4, kk, hs(a), s_rw.astype(f32))
    mu = jnp.mean(o, axis=-1, keepdims=True)
    var = jnp.mean(jnp.square(o - mu), axis=-1, keepdims=True)
    o = ((o - mu) * lax.rsqrt(var + RW_GN_EPS)).reshape(b, t, RW_C) * P('rw_ln_g') + P('rw_ln_b')
    bonus = jnp.sum(r4 * k4 * P('rw_r_k'), axis=-1, keepdims=True) * v4
    o_rw = (o + bonus.reshape(b, t, RW_C)) * (jax.nn.sigmoid(xg) @ P('rw_g_up'))

    mix = jnp.matmul(jnp.concatenate([o_gla, o_gdn, o_rw], axis=-1).astype(h.dtype), prm['w_out'][l])
    new = (s_gla_new.astype(s_gla.dtype), s_gdn_new.astype(s_gdn.dtype), c_gdn_new.astype(c_gdn.dtype),
           s_rw_new.astype(s_rw.dtype), c_rw_new.astype(c_rw.dtype))
    return mix, new, v_first


def trunk(x, s_gla, s_gdn, c_gdn, s_rw, c_rw, prm):
    outs = ([], [], [], [], [])
    v_first = None
    for l in range(DEPTH):
        h = rmsnorm(x, prm['norm1_g'][l])
        mix, new, v_first = mixer_layer(h, l, v_first, s_gla[l], s_gdn[l], c_gdn[l], s_rw[l], c_rw[l], prm)
        x = x + mix
        h = rmsnorm(x, prm['norm2_g'][l])
        x = x + jnp.matmul(jax.nn.silu(jnp.matmul(h, prm['w_ffn_gate'][l])) * jnp.matmul(h, prm['w_ffn_up'][l]),
                           prm['w_ffn_down'][l])
        for lst, arr in zip(outs, new):
            lst.append(arr)
    y = rmsnorm(x, prm['final_norm_g'])
    return y, [jnp.stack(lst) for lst in outs]


def setup_inputs(seed: int = 0) -> dict:
    key = jax.random.key(seed)
    ks = iter(jax.random.split(key, 48))
    f32 = jnp.float32
    nrm = lambda shape, scale: jax.random.normal(next(ks), shape, f32) * scale
    L = DEPTH
    dt = jnp.exp(jax.random.uniform(next(ks), (L, GDN_HEADS), f32, float(np.log(1e-3)), float(np.log(1e-1))))
    return {
        'x_prompt': nrm((BATCH, SEQ, D_MODEL), 1.0),
        'x_sample': nrm((DEC_BATCH, DEC_SEQ, D_MODEL), 1.0),
        'state_gla': nrm((L, DEC_BATCH, GLA_HEADS, GLA_DK, GLA_DV), 0.5),
        'state_gdn': nrm((L, DEC_BATCH, GDN_HEADS, GDN_DK, GDN_DV), 0.3),
        'cache_gdn_conv': nrm((L, DEC_BATCH, CONV_W - 1, GDN_CONV_CH), 1.0),
        'state_rwkv': nrm((L, DEC_BATCH, RW_HEADS, RW_N, RW_N), 0.3),
        'cache_rwkv_shift': nrm((L, DEC_BATCH, 1, RW_PROJ), 1.0),
        'norm1_g': 1.0 + nrm((L, D_MODEL), 0.02),
        'w_in': nrm((L, D_MODEL, PROJ_WIDTH), D_MODEL ** -0.5),
        'gla_a_up': nrm((L, GLA_LORA, GLA_QK), GLA_LORA ** -0.5),
        'gla_a_bias': 1.0 + nrm((L, GLA_QK), 0.5),
        'gla_norm_g': 1.0 + nrm((L, GLA_DV), 0.02),
        'gdn_conv_w': nrm((L, CONV_W, GDN_CONV_CH), CONV_W ** -0.5),
        'gdn_A_log': jnp.log(jax.random.uniform(next(ks), (L, GDN_HEADS), f32, 1.0, 16.0)),
        'gdn_dt_bias': dt + jnp.log(-jnp.expm1(-dt)),
        'gdn_norm_g': 1.0 + nrm((L, GDN_DV), 0.02),
        'rw_mu': jax.random.uniform(next(ks), (L, RW_PROJ), f32),
        'rw_w0': jax.random.uniform(next(ks), (L, RW_C), f32, -6.0, -1.0),
        'rw_w_up': nrm((L, RW_DECAY_LORA, RW_C), 0.1),
        'rw_a0': nrm((L, RW_C), 0.1),
        'rw_a_up': nrm((L, RW_AAA_LORA, RW_C), 0.1),
        'rw_v0': nrm((L - 1, RW_C), 0.1),
        'rw_v_down': nrm((L - 1, RW_C, RW_MV_LORA), RW_C ** -0.5),
        'rw_v_up': nrm((L - 1, RW_MV_LORA, RW_C), 0.1),
        'rw_g_up': nrm((L, RW_GATE_LORA, RW_C), RW_GATE_LORA ** -0.5),
        'rw_k_k': 0.85 + nrm((L, RW_C), 0.02),
        'rw_k_a': 1.0 + nrm((L, RW_C), 0.02),
        'rw_r_k': nrm((L, RW_HEADS, RW_N), 0.1),
        'rw_ln_g': 1.0 + nrm((L, RW_C), 0.02),
        'rw_ln_b': nrm((L, RW_C), 0.02),
        'w_out': nrm((L, MIX_WIDTH, D_MODEL), 0.5 * MIX_WIDTH ** -0.5),
        'norm2_g': 1.0 + nrm((L, D_MODEL), 0.02),
        'w_ffn_gate': nrm((L, D_MODEL, D_FF), D_MODEL ** -0.5),
        'w_ffn_up': nrm((L, D_MODEL, D_FF), D_MODEL ** -0.5),
        'w_ffn_down': nrm((L, D_FF, D_MODEL), 0.5 * D_FF ** -0.5),
        'final_norm_g': 1.0 + nrm((D_MODEL,), 0.02),
    }


def reference(x_prompt, x_sample, state_gla, state_gdn, cache_gdn_conv, state_rwkv, cache_rwkv_shift,
              norm1_g, w_in, gla_a_up, gla_a_bias, gla_norm_g, gdn_conv_w, gdn_A_log, gdn_dt_bias,
              gdn_norm_g, rw_mu, rw_w0, rw_w_up, rw_a0, rw_a_up, rw_v0, rw_v_down, rw_v_up, rw_g_up,
              rw_k_k, rw_k_a, rw_r_k, rw_ln_g, rw_ln_b, w_out, norm2_g, w_ffn_gate, w_ffn_up,
              w_ffn_down, final_norm_g):
    prm = dict(norm1_g=norm1_g, w_in=w_in, gla_a_up=gla_a_up, gla_a_bias=gla_a_bias, gla_norm_g=gla_norm_g,
               gdn_conv_w=gdn_conv_w, gdn_A_log=gdn_A_log, gdn_dt_bias=gdn_dt_bias, gdn_norm_g=gdn_norm_g,
               rw_mu=rw_mu, rw_w0=rw_w0, rw_w_up=rw_w_up, rw_a0=rw_a0, rw_a_up=rw_a_up, rw_v0=rw_v0,
               rw_v_down=rw_v_down, rw_v_up=rw_v_up, rw_g_up=rw_g_up, rw_k_k=rw_k_k, rw_k_a=rw_k_a,
               rw_r_k=rw_r_k, rw_ln_g=rw_ln_g, rw_ln_b=rw_ln_b, w_out=w_out, norm2_g=norm2_g,
               w_ffn_gate=w_ffn_gate, w_ffn_up=w_ffn_up, w_ffn_down=w_ffn_down, final_norm_g=final_norm_g)
    bp, dtp = x_prompt.shape[0], x_prompt.dtype
    y_prompt, (gla_p, gdn_p, conv_p, rwkv_p, shift_p) = trunk(
        x_prompt,
        jnp.zeros((DEPTH, bp, GLA_HEADS, GLA_DK, GLA_DV), dtp),
        jnp.zeros((DEPTH, bp, GDN_HEADS, GDN_DK, GDN_DV), dtp),
        jnp.zeros((DEPTH, bp, CONV_W - 1, GDN_CONV_CH), dtp),
        jnp.zeros((DEPTH, bp, RW_HEADS, RW_N, RW_N), dtp),
        jnp.zeros((DEPTH, bp, 1, RW_PROJ), dtp),
        prm)
    y_sample, (gla_s, gdn_s, conv_s, rwkv_s, shift_s) = trunk(
        x_sample, state_gla, state_gdn, cache_gdn_conv, state_rwkv, cache_rwkv_shift, prm)
    return (y_prompt, y_sample, gla_p, gdn_p, conv_p, rwkv_p, shift_p, gla_s, gdn_s, conv_s, rwkv_s, shift_s)
```

```python
import functools

import numpy as np
import jax
import jax.numpy as jnp
from jax import lax
from jax.experimental import pallas as pl
from jax.experimental.pallas import tpu as pltpu

D_MODEL = 2048
DEPTH = 4
CHUNK = 64
NORM_EPS = 1e-6
L2_EPS = 1e-6
GLA_HEADS, GLA_DK, GLA_DV, GLA_LORA, GLA_GATE_TEMP = 4, 64, 128, 16, 16.0
GDN_HEADS, GDN_DK, GDN_DV, CONV_W = 6, 128, 128, 4
RW_HEADS, RW_N = 12, 64
RW_DECAY_LORA, RW_AAA_LORA, RW_MV_LORA, RW_GATE_LORA = 64, 64, 32, 128
RW_GN_EPS = 64e-5

GLA_QK = GLA_HEADS * GLA_DK
GLA_V = GLA_HEADS * GLA_DV
GDN_QK = GDN_HEADS * GDN_DK
GDN_V = GDN_HEADS * GDN_DV
GDN_CONV_CH = 2 * GDN_QK + GDN_V
RW_C = RW_HEADS * RW_N
GLA_PROJ = 2 * GLA_QK + GLA_V + GLA_LORA + GLA_V
GDN_PROJ = GDN_CONV_CH + 2 * GDN_HEADS + GDN_V
RW_PROJ = 3 * RW_C + RW_DECAY_LORA + RW_AAA_LORA + RW_GATE_LORA
D_FF = -(-8 * D_MODEL // (3 * 256)) * 256

LANES = 128
ZG_W = 2 * GLA_QK + 2 * GLA_V
ZD_W = GDN_CONV_CH + GDN_V
ZS_W = LANES
ZS_BETA = GLA_LORA
ZS_DT = GLA_LORA + GDN_HEADS
VMEM_LIMIT = 52 * 1024 * 1024

_BF = jnp.bfloat16
_F32 = jnp.float32
_NEG = -1e30


def _dot(a, b):
    return jnp.dot(a.astype(_BF), b.astype(_BF), preferred_element_type=_F32)


def _dot_nt(a, b):
    return lax.dot_general(a.astype(_BF), b.astype(_BF), (((1,), (1,)), ((), ())),
                           preferred_element_type=_F32)


def _dot_tn(a, b):
    return lax.dot_general(a.astype(_BF), b.astype(_BF), (((0,), (0,)), ((), ())),
                           preferred_element_type=_F32)


def _split3(x):
    hi = x.astype(_BF)
    r1 = x - hi.astype(_F32)
    mid = r1.astype(_BF)
    lo = (r1 - mid.astype(_F32)).astype(_BF)
    return hi, mid, lo


def _cdot(c, x):
    return sum(jnp.dot(c, p, preferred_element_type=_F32) for p in _split3(x))


def _cdot_tn(x, c):
    return sum(lax.dot_general(p, c, (((0,), (0,)), ((), ())), preferred_element_type=_F32)
               for p in _split3(x))


def _dot_hi(a, b):
    ah = a.astype(_BF)
    al = (a - ah.astype(_F32)).astype(_BF)
    bh = b.astype(_BF)
    bl = (b - bh.astype(_F32)).astype(_BF)
    d = functools.partial(jnp.dot, preferred_element_type=_F32)
    return d(ah, bh) + d(ah, bl) + d(al, bh)


def _sigmoid(x):
    return 1.0 / (1.0 + jnp.exp(-x))


def _silu(x):
    return x * _sigmoid(x)


def _softplus(x):
    return jnp.maximum(x, 0.0) + jnp.log(1.0 + jnp.exp(-jnp.abs(x)))


def _unit_lower_inverse(low, n):
    rows = lax.broadcasted_iota(jnp.int32, (n, n), 0)
    cols = lax.broadcasted_iota(jnp.int32, (n, n), 1)
    p = -low
    t = jnp.where(rows == cols, 1.0, 0.0) + p
    k = 2
    while k < n:
        p = _dot_hi(p, p)
        t = t + _dot_hi(t, p)
        k *= 2
    return t


def _lane_halves(shape):
    lane = lax.broadcasted_iota(jnp.int32, shape, len(shape) - 1)
    return lane < (LANES // 2)


def _half_sum(x, half):
    s0 = jnp.sum(jnp.where(half, x, 0.0), axis=-1, keepdims=True)
    s1 = jnp.sum(jnp.where(half, 0.0, x), axis=-1, keepdims=True)
    return jnp.where(half, s0, s1)


def _rmsnorm_rows(x, g):
    return x * lax.rsqrt(jnp.mean(x * x, axis=-1, keepdims=True) + NORM_EPS) * g


def _params(*sem):
    return pltpu.CompilerParams(dimension_semantics=sem, vmem_limit_bytes=VMEM_LIMIT)


def _norm_kernel(x_ref, g_ref, o_ref):
    o_ref[...] = _rmsnorm_rows(x_ref[...], g_ref[...]).astype(o_ref.dtype)


def _norm_call(x, g, tm):
    m, d = x.shape
    return pl.pallas_call(
        _norm_kernel,
        grid=(m // tm,),
        in_specs=[pl.BlockSpec((tm, d), lambda i: (i, 0)), pl.BlockSpec((1, d), lambda i: (0, 0))],
        out_specs=pl.BlockSpec((tm, d), lambda i: (i, 0)),
        out_shape=jax.ShapeDtypeStruct((m, d), _BF),
        compiler_params=_params("arbitrary"),
        name="rmsnorm",
    )(x, g)


def _mm_kernel(a_ref, w_ref, o_ref):
    o_ref[...] = jnp.dot(a_ref[...], w_ref[...], preferred_element_type=_F32)


def _mm_call(a, w, tm, tn, name):
    m, k = a.shape
    n = w.shape[1]
    return pl.pallas_call(
        _mm_kernel,
        grid=(n // tn, m // tm),
        in_specs=[pl.BlockSpec((tm, k), lambda j, i: (i, 0)), pl.BlockSpec((k, tn), lambda j, i: (0, j))],
        out_specs=pl.BlockSpec((tm, tn), lambda j, i: (i, j)),
        out_shape=jax.ShapeDtypeStruct((m, n), _F32),
        compiler_params=_params("arbitrary", "arbitrary"),
        name=name,
    )(a, w)


def _outproj_kernel(og_ref, od_ref, or_ref, w_ref, x_ref, g_ref, x1_ref, h_ref):
    d = functools.partial(jnp.dot, preferred_element_type=_F32)
    acc = (d(og_ref[...], w_ref[0:GLA_V, :]) + d(od_ref[...], w_ref[GLA_V:GLA_V + GDN_V, :])
           + d(or_ref[...], w_ref[GLA_V + GDN_V:, :]))
    x1 = x_ref[...] + acc
    x1_ref[...] = x1
    h_ref[...] = _rmsnorm_rows(x1, g_ref[...]).astype(h_ref.dtype)


def _outproj_call(og, od, orw, w, x, g, tm):
    m, d = x.shape
    row = lambda width: pl.BlockSpec((tm, width), lambda i: (i, 0))
    return pl.pallas_call(
        _outproj_kernel,
        grid=(m // tm,),
        in_specs=[row(GLA_V), row(GDN_V), row(RW_C), pl.BlockSpec(w.shape, lambda i: (0, 0)), row(d),
                  pl.BlockSpec((1, d), lambda i: (0, 0))],
        out_specs=[row(d), row(d)],
        out_shape=[jax.ShapeDtypeStruct((m, d), _F32), jax.ShapeDtypeStruct((m, d), _BF)],
        compiler_params=_params("arbitrary"),
        name="outproj",
    )(og, od, orw, w, x, g)


def _ffn_kernel(h_ref, wg_ref, wu_ref, wd_ref, x_ref, g_ref, *rest, nf, emit_x):
    if emit_x:
        x2_ref, hn_ref, acc_ref = rest
    else:
        hn_ref, acc_ref = rest
    f = pl.program_id(1)

    @pl.when(f == 0)
    def _():
        acc_ref[...] = jnp.zeros_like(acc_ref)

    h = h_ref[...]
    gate = jnp.dot(h, wg_ref[...], preferred_element_type=_F32)
    up = jnp.dot(h, wu_ref[...], preferred_element_type=_F32)
    act = (_silu(gate) * up).astype(_BF)
    acc_ref[...] += jnp.dot(act, wd_ref[...], preferred_element_type=_F32)

    @pl.when(f == nf - 1)
    def _():
        x2 = x_ref[...] + acc_ref[...]
        if emit_x:
            x2_ref[...] = x2
        hn_ref[...] = _rmsnorm_rows(x2, g_ref[...]).astype(hn_ref.dtype)


def _ffn_call(h, wg, wu, wd, x, g, tm, tf, last):
    m, d = x.shape
    nf = D_FF // tf
    row = pl.BlockSpec((tm, d), lambda i, f: (i, 0))
    if last:
        out_specs = [row]
        out_shape = [jax.ShapeDtypeStruct((m, d), _F32)]
    else:
        out_specs = [row, row]
        out_shape = [jax.ShapeDtypeStruct((m, d), _F32), jax.ShapeDtypeStruct((m, d), _BF)]
    return pl.pallas_call(
        functools.partial(_ffn_kernel, nf=nf, emit_x=not last),
        grid=(m // tm, nf),
        in_specs=[row, pl.BlockSpec((d, tf), lambda i, f: (0, f)), pl.BlockSpec((d, tf), lambda i, f: (0, f)),
                  pl.BlockSpec((tf, d), lambda i, f: (f, 0)), row, pl.BlockSpec((1, d), lambda i, f: (0, 0))],
        out_specs=out_specs,
        out_shape=out_shape,
        scratch_shapes=[pltpu.VMEM((tm, d), _F32)],
        compiler_params=_params("arbitrary", "arbitrary"),
        name="ffn",
    )(h, wg, wu, wd, x, g)


def _gla_consts(c):
    nlev = int(np.log2(c))
    t = np.arange(c)
    blocks = [t[:, None] >= t[None, :],
              t[None, :] > t[:, None]]
    mq, mk, masks = [], [], []
    for lv in range(nlev):
        m = c >> lv
        half = m // 2
        blk, pos = t // m, t % m
        mid = blk * m + half
        upper = pos >= half
        mq.append(upper[:, None] & (t[None, :] >= mid[:, None]) & (t[None, :] <= t[:, None]))
        mk.append((~upper)[:, None] & (t[None, :] > t[:, None]) & (t[None, :] <= mid[:, None] - 1))
        masks.append((blk[:, None] == blk[None, :]) & upper[:, None] & (~upper)[None, :])
    mc = np.concatenate(blocks + mq + mk, axis=0).astype(np.float32)
    return jnp.asarray(mc, _BF), jnp.asarray(np.stack(masks).astype(np.float32)), nlev


def _gla_kernel(zg_ref, zs_ref, aup_ref, abias_ref, gn_ref, s0_ref, mc_ref, mk_ref,
                o_ref, sout_ref, s_scr, *, c, nlev, nc):
    step = pl.program_id(1)

    @pl.when(step == 0)
    def _():
        s_scr[...] = s0_ref[0]

    xa = _dot(zs_ref[...], aup_ref[...]) + abias_ref[...]
    la = (jnp.minimum(xa, 0.0) - jnp.log(1.0 + jnp.exp(-jnp.abs(xa)))) * (1.0 / GLA_GATE_TEMP)
    ex = jnp.exp(_cdot(mc_ref[...], la))
    q = zg_ref[:, 0:GLA_QK] * (GLA_DK ** -0.5)
    k = zg_ref[:, GLA_QK:2 * GLA_QK]
    qd = q * ex[0:c]
    kd = k * ex[c:2 * c]
    half = _lane_halves((c, LANES))
    eye = lax.broadcasted_iota(jnp.int32, (c, c), 0) == lax.broadcasted_iota(jnp.int32, (c, c), 1)
    ones = jnp.ones((c, LANES), _BF)
    for p in range(GLA_HEADS // 2):
        sl = slice(LANES * p, LANES * (p + 1))
        q_p, k_p = q[:, sl], k[:, sl]
        qx = [q_p * ex[(2 + lv) * c:(3 + lv) * c, sl] for lv in range(nlev)]
        kx = [k_p * ex[(2 + nlev + lv) * c:(3 + nlev + lv) * c, sl] for lv in range(nlev)]
        s_p = s_scr[p]
        s_new = jnp.exp(_cdot_tn(la[:, sl], ones)) * s_p
        for e in range(2):
            h = 2 * p + e
            sel = (lambda x: jnp.where(half, x, 0.0)) if e == 0 else (lambda x: jnp.where(half, 0.0, x))
            v_h = zg_ref[:, 2 * GLA_QK + GLA_DV * h:2 * GLA_QK + GLA_DV * (h + 1)]
            g_h = zg_ref[:, 2 * GLA_QK + GLA_V + GLA_DV * h:2 * GLA_QK + GLA_V + GLA_DV * (h + 1)]
            a = jnp.where(eye, _dot_nt(sel(q_p), k_p), 0.0)
            for lv in range(nlev):
                a = a + mk_ref[lv] * _dot_nt(sel(qx[lv]), kx[lv])
            o = _dot(sel(qd[:, sl]), s_p) + _dot(a, v_h)
            s_new = s_new + _dot_tn(sel(kd[:, sl]), v_h)
            y = _rmsnorm_rows(o, gn_ref[...]) * _silu(g_h)
            o_ref[:, GLA_DV * h:GLA_DV * (h + 1)] = y.astype(o_ref.dtype)
        s_scr[p] = s_new

    @pl.when(step == nc - 1)
    def _():
        sout_ref[0] = s_scr[...]


def _gla_call(zg, zs, aup, abias, gn, s0, b, t):
    c = min(CHUNK, t)
    nc = t // c
    mc, masks, nlev = _gla_consts(c)
    full = lambda a: pl.BlockSpec(a.shape, lambda i, j: (0,) * a.ndim)
    st = pl.BlockSpec((1,) + s0.shape[1:], lambda i, j: (i, 0, 0, 0))
    tok = lambda w: pl.BlockSpec((c, w), lambda i, j: (i * nc + j, 0))
    return pl.pallas_call(
        functools.partial(_gla_kernel, c=c, nlev=nlev, nc=nc),
        grid=(b, nc),
        in_specs=[tok(ZG_W), tok(ZS_W), full(aup), full(abias), full(gn), st, full(mc), full(masks)],
        out_specs=[tok(GLA_V), st],
        out_shape=[jax.ShapeDtypeStruct((b * t, GLA_V), _BF), jax.ShapeDtypeStruct(s0.shape, _F32)],
        scratch_shapes=[pltpu.VMEM(s0.shape[1:], _F32)],
        compiler_params=_params("arbitrary", "arbitrary"),
        name="gla",
    )(zg, zs, aup, abias, gn, s0, mc, masks)


def _gdn_kernel(zd_ref, zs_ref, cw_ref, alog_ref, dtb_ref, gn_ref, cache_ref, s0_ref, tril_ref, triu_ref, mlast_ref,
                o_ref, cout_ref, sout_ref, cbuf, s_scr, *, c, nc):
    step = pl.program_id(1)
    tail0 = 8 - (CONV_W - 1)

    @pl.when(step == 0)
    def _():
        cbuf[tail0:8, :] = cache_ref[0]
        s_scr[...] = s0_ref[0]

    x = zd_ref[:, 0:GDN_CONV_CH]
    cbuf[8:8 + c, :] = x
    y = cbuf[tail0:tail0 + c, :] * cw_ref[0:1, :]
    for j in range(1, CONV_W - 1):
        y = y + cbuf[tail0 + j:tail0 + j + c, :] * cw_ref[j:j + 1, :]
    y = y + x * cw_ref[CONV_W - 1:CONV_W, :]
    tail = cbuf[c + tail0:c + 8, :]
    cbuf[tail0:8, :] = tail

    @pl.when(step == nc - 1)
    def _():
        cout_ref[0] = tail

    ys = _silu(y)
    zs = zs_ref[...]
    lg = -jnp.exp(alog_ref[...]) * _softplus(zs + dtb_ref[...])
    bt = _sigmoid(zs)
    bcum = _cdot(tril_ref[...], lg)
    bcum_t = _cdot_tn(lg, triu_ref[...])
    bdl = _cdot(mlast_ref[...], lg)
    btot_t = _cdot_tn(lg, jnp.ones((c, LANES), _BF))
    lane = lax.broadcasted_iota(jnp.int32, (c, LANES), 1)
    rows = lax.broadcasted_iota(jnp.int32, (c, c), 0)
    cols = lax.broadcasted_iota(jnp.int32, (c, c), 1)
    for h in range(GDN_HEADS):
        bcol = jnp.sum(jnp.where(lane == ZS_DT + h, bcum, 0.0), axis=1, keepdims=True)
        beta = jnp.sum(jnp.where(lane == ZS_BETA + h, bt, 0.0), axis=1, keepdims=True)
        dlcol = jnp.sum(jnp.where(lane == ZS_DT + h, bdl, 0.0), axis=1, keepdims=True)
        brow = bcum_t[ZS_DT + h:ZS_DT + h + 1, :]
        elast = jnp.exp(btot_t[ZS_DT + h:ZS_DT + h + 1, :])
        dec = jnp.exp(jnp.where(rows >= cols, bcol - brow, _NEG))
        eb = jnp.exp(bcol)
        q = ys[:, GDN_DK * h:GDN_DK * (h + 1)]
        k = ys[:, GDN_QK + GDN_DK * h:GDN_QK + GDN_DK * (h + 1)]
        v = ys[:, 2 * GDN_QK + GDN_DV * h:2 * GDN_QK + GDN_DV * (h + 1)]
        q = q * lax.rsqrt(jnp.sum(q * q, axis=-1, keepdims=True) + L2_EPS) * (GDN_DK ** -0.5)
        k = k * lax.rsqrt(jnp.sum(k * k, axis=-1, keepdims=True) + L2_EPS)
        qk = jnp.concatenate([q, k], axis=0)
        sc = _dot_nt(qk, k)
        s_h = s_scr[h]
        ps = _dot(qk, s_h)
        low = jnp.where(rows > cols, beta * dec * sc[c:], 0.0)
        rhs = beta * (v - eb * ps[c:])
        delta = _dot_hi(_unit_lower_inverse(low, c), rhs)
        o = eb * ps[:c] + _dot(sc[:c] * dec, delta)
        s_scr[h] = elast * s_h + _dot_tn(k * jnp.exp(dlcol), delta)
        g_h = zd_ref[:, GDN_CONV_CH + GDN_DV * h:GDN_CONV_CH + GDN_DV * (h + 1)]
        y_h = _rmsnorm_rows(o, gn_ref[...]) * _silu(g_h)
        o_ref[:, GDN_DV * h:GDN_DV * (h + 1)] = y_h.astype(o_ref.dtype)

    @pl.when(step == nc - 1)
    def _():
        sout_ref[0] = s_scr[...]


def _tri_consts(c):
    t = np.arange(c)
    tril = (t[:, None] >= t[None, :]).astype(np.float32)
    mlast = (t[None, :] > t[:, None]).astype(np.float32)
    return jnp.asarray(tril, _BF), jnp.asarray(tril.T, _BF), jnp.asarray(mlast, _BF)


def _gdn_call(zd, zs, cw, alog, dtb, gn, cache, s0, b, t):
    c = min(CHUNK, t)
    nc = t // c
    tril, triu, mlast = _tri_consts(c)
    full = lambda a: pl.BlockSpec(a.shape, lambda i, j: (0,) * a.ndim)
    st = pl.BlockSpec((1,) + s0.shape[1:], lambda i, j: (i, 0, 0, 0))
    ch = pl.BlockSpec((1,) + cache.shape[1:], lambda i, j: (i, 0, 0))
    tok = lambda w: pl.BlockSpec((c, w), lambda i, j: (i * nc + j, 0))
    return pl.pallas_call(
        functools.partial(_gdn_kernel, c=c, nc=nc),
        grid=(b, nc),
        in_specs=[tok(ZD_W), tok(ZS_W), full(cw), full(alog), full(dtb), full(gn), ch, st, full(tril), full(triu),
                  full(mlast)],
        out_specs=[tok(GDN_V), ch, st],
        out_shape=[jax.ShapeDtypeStruct((b * t, GDN_V), _BF), jax.ShapeDtypeStruct(cache.shape, _F32),
                   jax.ShapeDtypeStruct(s0.shape, _F32)],
        scratch_shapes=[pltpu.VMEM((c + 8, GDN_CONV_CH), _F32), pltpu.VMEM(s0.shape[1:], _F32)],
        compiler_params=_params("arbitrary", "arbitrary"),
        name="gdn",
    )(zd, zs, cw, alog, dtb, gn, cache, s0, tril, triu, mlast)


def _rw_kernel(*refs, c, nc, first):
    if first:
        (zr_ref, mu_ref, w0_ref, wup_ref, a0_ref, aup_ref, gup_ref, kk_ref, ka_ref, rk_ref, lng_ref, lnb_ref,
         cache_ref, s0_ref, tril_ref, mlast_ref, o_ref, vf_out_ref, shout_ref, sout_ref, sbuf, s_scr) = refs
    else:
        (zr_ref, vf_ref, v0_ref, vdown_ref, vup_ref, mu_ref, w0_ref, wup_ref, a0_ref, aup_ref, gup_ref, kk_ref,
         ka_ref, rk_ref, lng_ref, lnb_ref, cache_ref, s0_ref, tril_ref, mlast_ref, o_ref, shout_ref, sout_ref,
         sbuf, s_scr) = refs
    step = pl.program_id(1)

    @pl.when(step == 0)
    def _():
        sbuf[7:8, :] = cache_ref[0]
        s_scr[...] = s0_ref[0]

    x = zr_ref[...]
    sbuf[8:8 + c, :] = x
    zprev = sbuf[7:7 + c, :]
    last = sbuf[c + 7:c + 8, :]
    sbuf[7:8, :] = last

    @pl.when(step == nc - 1)
    def _():
        shout_ref[0] = last

    zm = x + (zprev - x) * mu_ref[...]
    xr, xk, xv = zm[:, 0:RW_C], zm[:, RW_C:2 * RW_C], zm[:, 2 * RW_C:3 * RW_C]
    xwa = zm[:, 3 * RW_C:3 * RW_C + LANES]
    xg = zm[:, 3 * RW_C + LANES:]
    wlog = -_softplus(-(w0_ref[...] + _dot(jnp.tanh(xwa), wup_ref[...]))) - 0.5
    lw = -jnp.exp(wlog)
    a = _sigmoid(a0_ref[...] + _dot(xwa, aup_ref[...]))
    if first:
        vf_out_ref[...] = xv
    else:
        nu = _sigmoid(v0_ref[...] + _dot(_dot(xv, vdown_ref[...]), vup_ref[...]))
        xv = xv + (vf_ref[...] - xv) * nu
    kkp = xk * kk_ref[...]
    xk2 = xk * (1.0 + (a - 1.0) * ka_ref[...])
    gate = _dot(_sigmoid(xg), gup_ref[...])
    cum = _cdot(tril_ref[...], lw)
    e_c = jnp.exp(cum)
    e_cp = jnp.exp(cum - lw)
    e_nc = jnp.exp(-cum)
    e_dl = jnp.exp(_cdot(mlast_ref[...], lw))
    half = _lane_halves((c, LANES))
    half2 = _lane_halves((2 * c, LANES))
    rows = lax.broadcasted_iota(jnp.int32, (c, c), 0)
    cols = lax.broadcasted_iota(jnp.int32, (c, c), 1)
    srow = lax.broadcasted_iota(jnp.int32, (LANES, LANES), 0) < (LANES // 2)
    scol = lax.broadcasted_iota(jnp.int32, (LANES, LANES), 1) < (LANES // 2)
    same_head = srow == scol
    ones = jnp.ones((c, LANES), _BF)
    for p in range(RW_HEADS // 2):
        sl = slice(LANES * p, LANES * (p + 1))
        r_p, k_p, v_p, a_p = xr[:, sl], xk2[:, sl], xv[:, sl], a[:, sl]
        kap = kkp[:, sl]
        kap = kap * lax.rsqrt(_half_sum(kap * kap, half) + L2_EPS)
        ahat = -(kap * a_p)
        x2 = jnp.concatenate([kap * e_cp[:, sl], r_p * e_c[:, sl]], axis=0)
        at = ahat * e_nc[:, sl]
        kt = k_p * e_nc[:, sl]
        s_bd = s_scr[p]
        ps = _dot(x2, s_bd)
        ak_v, tinv, ara, ark = [], [], [], []
        for e in range(2):
            xm = jnp.where(half2, x2, 0.0) if e == 0 else jnp.where(half2, 0.0, x2)
            aa = _dot_nt(xm, at)
            ak = _dot_nt(xm, kt)
            ak_v.append(_dot(jnp.where(rows > cols, ak[:c], 0.0), v_p))
            tinv.append(_unit_lower_inverse(jnp.where(rows > cols, -aa[:c], 0.0), c))
            ara.append(jnp.where(rows >= cols, aa[c:], 0.0))
            ark.append(jnp.where(rows >= cols, ak[c:], 0.0))
        rhs = ps[:c] + jnp.where(half, ak_v[0], ak_v[1])
        u = jnp.where(half, _dot_hi(tinv[0], rhs), _dot_hi(tinv[1], rhs))
        o = ps[c:] + jnp.where(half, _dot(ara[0], u) + _dot(ark[0], v_p), _dot(ara[1], u) + _dot(ark[1], v_p))
        upd = _dot_tn(ahat * e_dl[:, sl], u) + _dot_tn(k_p * e_dl[:, sl], v_p)
        s_scr[p] = jnp.exp(_cdot_tn(lw[:, sl], ones)) * s_bd + jnp.where(same_head, upd, 0.0)
        mean = _half_sum(o, half) * (1.0 / RW_N)
        dev = o - mean
        var = _half_sum(dev * dev, half) * (1.0 / RW_N)
        on = dev * lax.rsqrt(var + RW_GN_EPS) * lng_ref[:, sl] + lnb_ref[:, sl]
        bonus = _half_sum(r_p * k_p * rk_ref[:, sl], half) * v_p
        o_ref[:, sl] = ((on + bonus) * gate[:, sl]).astype(o_ref.dtype)

    @pl.when(step == nc - 1)
    def _():
        sout_ref[0] = s_scr[...]


def _rw_call(zr, vf, vmix, prm, cache, s0, b, t):
    c = min(CHUNK, t)
    nc = t // c
    tril, _, mlast = _tri_consts(c)
    first = vf is None
    full = lambda a: pl.BlockSpec(a.shape, lambda i, j: (0,) * a.ndim)
    st = pl.BlockSpec((1,) + s0.shape[1:], lambda i, j: (i, 0, 0, 0))
    ch = pl.BlockSpec((1,) + cache.shape[1:], lambda i, j: (i, 0, 0))
    tok = lambda w: pl.BlockSpec((c, w), lambda i, j: (i * nc + j, 0))
    m = b * t
    args = [zr] + ([] if first else [vf] + list(vmix)) + list(prm) + [cache, s0, tril, mlast]
    in_specs = ([tok(RW_PROJ)] + ([] if first else [tok(RW_C)] + [full(a) for a in vmix])
                + [full(a) for a in prm] + [ch, st, full(tril), full(mlast)])
    out_specs = [tok(RW_C)] + ([tok(RW_C)] if first else []) + [ch, st]
    out_shape = ([jax.ShapeDtypeStruct((m, RW_C), _BF)] + ([jax.ShapeDtypeStruct((m, RW_C), _F32)] if first else [])
                 + [jax.ShapeDtypeStruct(cache.shape, _F32), jax.ShapeDtypeStruct(s0.shape, _F32)])
    return pl.pallas_call(
        functools.partial(_rw_kernel, c=c, nc=nc, first=first),
        grid=(b, nc),
        in_specs=in_specs,
        out_specs=out_specs,
        out_shape=out_shape,
        scratch_shapes=[pltpu.VMEM((c + 8, RW_PROJ), _F32), pltpu.VMEM(s0.shape[1:], _F32)],
        compiler_params=_params("arbitrary", "arbitrary"),
        name="rwkv",
    )(*args)


def _prep_weights(p):
    w_in = p["w_in"]
    o1 = GLA_PROJ
    o2 = GLA_PROJ + GDN_PROJ
    lora0 = 2 * GLA_QK + GLA_V
    w_gla = jnp.concatenate([w_in[:, :, 0:lora0], w_in[:, :, lora0 + GLA_LORA:o1]], axis=2)
    w_gdn = jnp.concatenate([w_in[:, :, o1:o1 + GDN_CONV_CH], w_in[:, :, o1 + GDN_CONV_CH + 2 * GDN_HEADS:o2]], axis=2)
    w_rw = w_in[:, :, o2:]
    narrow = jnp.concatenate([w_in[:, :, lora0:lora0 + GLA_LORA],
                              w_in[:, :, o1 + GDN_CONV_CH:o1 + GDN_CONV_CH + 2 * GDN_HEADS]], axis=2)
    w_small = jnp.pad(narrow, ((0, 0), (0, 0), (0, ZS_W - narrow.shape[2])))
    bf = lambda a: a.astype(_BF)
    pad_rows = lambda a, lo, hi: jnp.pad(a, ((0, 0), (lo, hi), (0, 0)))
    pad_lanes = lambda a, lo: jnp.pad(a, ((0, 0), (lo, ZS_W - lo - a.shape[1])))[:, None, :]
    row = lambda a: a.reshape(a.shape[0], 1, -1)
    return dict(
        w_gla=bf(w_gla), w_gdn=bf(w_gdn), w_rw=bf(w_rw), w_small=bf(w_small),
        w_out=bf(p["w_out"]), w_g=bf(p["w_ffn_gate"]), w_u=bf(p["w_ffn_up"]), w_d=bf(p["w_ffn_down"]),
        gla_aup=bf(pad_rows(p["gla_a_up"], 0, ZS_W - GLA_LORA)), gla_abias=row(p["gla_a_bias"]),
        gla_gn=row(p["gla_norm_g"]),
        gdn_cw=p["gdn_conv_w"], gdn_alog=pad_lanes(p["gdn_A_log"], ZS_DT), gdn_dtb=pad_lanes(p["gdn_dt_bias"], ZS_DT),
        gdn_gn=row(p["gdn_norm_g"]),
        rw_mu=row(p["rw_mu"]), rw_w0=row(p["rw_w0"]), rw_wup=bf(pad_rows(p["rw_w_up"], 0, LANES - RW_DECAY_LORA)),
        rw_a0=row(p["rw_a0"]), rw_aup=bf(pad_rows(p["rw_a_up"], RW_DECAY_LORA, 0)), rw_gup=bf(p["rw_g_up"]),
        rw_kk=row(p["rw_k_k"]), rw_ka=row(p["rw_k_a"]), rw_rk=p["rw_r_k"].reshape(DEPTH, 1, RW_C),
        rw_lng=row(p["rw_ln_g"]), rw_lnb=row(p["rw_ln_b"]),
        rw_v0=row(p["rw_v0"]), rw_vdown=bf(p["rw_v_down"]), rw_vup=bf(p["rw_v_up"]),
        norm1=row(p["norm1_g"]), norm2=row(p["norm2_g"]), final=p["final_norm_g"].reshape(1, -1),
    )


def _pair_block_diag(s):
    b, h, n, _ = s.shape
    s = s.reshape(b, h // 2, 2, n, n)
    z = jnp.zeros_like(s[:, :, 0])
    top = jnp.concatenate([s[:, :, 0], z], axis=-1)
    bot = jnp.concatenate([z, s[:, :, 1]], axis=-1)
    return jnp.concatenate([top, bot], axis=-2)


def _pair_blocks(s):
    b, hp, n2, _ = s.shape
    n = n2 // 2
    return jnp.stack([s[:, :, :n, :n], s[:, :, n:, n:]], axis=2).reshape(b, 2 * hp, n, n)


def _trunk(x, s_gla, s_gdn, c_gdn, s_rw, c_rw, w):
    b, t, d = x.shape
    m = b * t
    tm = min(512, m)
    xf = x.reshape(m, d)
    h = _norm_call(xf, w["norm1"][0], tm)
    outs = ([], [], [], [], [])
    vf = None
    for l in range(DEPTH):
        zg = _mm_call(h, w["w_gla"][l], tm, ZG_W // 2, "proj_gla")
        zd = _mm_call(h, w["w_gdn"][l], tm, ZD_W // 2, "proj_gdn")
        zr = _mm_call(h, w["w_rw"][l], tm, RW_PROJ // 2, "proj_rw")
        zs = _mm_call(h, w["w_small"][l], tm, ZS_W, "proj_small")
        og, sg = _gla_call(zg, zs, w["gla_aup"][l], w["gla_abias"][l], w["gla_gn"][l],
                           s_gla[l].reshape(b, GLA_HEADS // 2, 2 * GLA_DK, GLA_DV), b, t)
        od, cg, sd = _gdn_call(zd, zs, w["gdn_cw"][l], w["gdn_alog"][l], w["gdn_dtb"][l], w["gdn_gn"][l],
                               c_gdn[l], s_gdn[l], b, t)
        prm = [w[n][l] for n in ("rw_mu", "rw_w0", "rw_wup", "rw_a0", "rw_aup", "rw_gup", "rw_kk", "rw_ka",
                                 "rw_rk", "rw_lng", "rw_lnb")]
        vmix = None if l == 0 else [w[n][l - 1] for n in ("rw_v0", "rw_vdown", "rw_vup")]
        res = _rw_call(zr, vf, vmix, prm, c_rw[l], _pair_block_diag(s_rw[l]), b, t)
        if l == 0:
            orw, vf, cr, sr = res
        else:
            orw, cr, sr = res
        xf, h2 = _outproj_call(og, od, orw, w["w_out"][l], xf, w["norm2"][l], min(256, m))
        last = l == DEPTH - 1
        g_next = w["final"] if last else w["norm1"][l + 1]
        res = _ffn_call(h2, w["w_g"][l], w["w_u"][l], w["w_d"][l], xf, g_next, tm, 512, last)
        if last:
            (y,) = res
        else:
            xf, h = res
        for lst, arr in zip(outs, (sg.reshape(b, GLA_HEADS, GLA_DK, GLA_DV), sd, cg, _pair_blocks(sr), cr)):
            lst.append(arr)
    return y.reshape(b, t, d), [jnp.stack(lst) for lst in outs]


def kernel(x_prompt, x_sample, state_gla, state_gdn, cache_gdn_conv, state_rwkv, cache_rwkv_shift, norm1_g, w_in, gla_a_up, gla_a_bias, gla_norm_g, gdn_conv_w, gdn_A_log, gdn_dt_bias, gdn_norm_g, rw_mu, rw_w0, rw_w_up, rw_a0, rw_a_up, rw_v0, rw_v_down, rw_v_up, rw_g_up, rw_k_k, rw_k_a, rw_r_k, rw_ln_g, rw_ln_b, w_out, norm2_g, w_ffn_gate, w_ffn_up, w_ffn_down, final_norm_g):
    p = dict(norm1_g=norm1_g, w_in=w_in, gla_a_up=gla_a_up, gla_a_bias=gla_a_bias, gla_norm_g=gla_norm_g,
             gdn_conv_w=gdn_conv_w, gdn_A_log=gdn_A_log, gdn_dt_bias=gdn_dt_bias, gdn_norm_g=gdn_norm_g,
             rw_mu=rw_mu, rw_w0=rw_w0, rw_w_up=rw_w_up, rw_a0=rw_a0, rw_a_up=rw_a_up, rw_v0=rw_v0,
             rw_v_down=rw_v_down, rw_v_up=rw_v_up, rw_g_up=rw_g_up, rw_k_k=rw_k_k, rw_k_a=rw_k_a,
             rw_r_k=rw_r_k, rw_ln_g=rw_ln_g, rw_ln_b=rw_ln_b, w_out=w_out, norm2_g=norm2_g,
             w_ffn_gate=w_ffn_gate, w_ffn_up=w_ffn_up, w_ffn_down=w_ffn_down, final_norm_g=final_norm_g)
    w = _prep_weights(p)
    bp = x_prompt.shape[0]
    zeros = lambda a: jnp.zeros((DEPTH, bp) + a.shape[2:], a.dtype)
    y_p, st_p = _trunk(x_prompt, zeros(state_gla), zeros(state_gdn), zeros(cache_gdn_conv), zeros(state_rwkv),
                       zeros(cache_rwkv_shift), w)
    y_s, st_s = _trunk(x_sample, state_gla, state_gdn, cache_gdn_conv, state_rwkv, cache_rwkv_shift, w)
    return (y_p, y_s, *st_p, *st_s)
```

```python
import functools

import numpy as np
import jax
import jax.numpy as jnp
from jax import lax
from jax.experimental import pallas as pl
from jax.experimental.pallas import tpu as pltpu

D_MODEL = 2048
DEPTH = 4
CHUNK = 64
NORM_EPS = 1e-6
L2_EPS = 1e-6
GLA_HEADS, GLA_DK, GLA_DV, GLA_LORA, GLA_GATE_TEMP = 4, 64, 128, 16, 16.0
GDN_HEADS, GDN_DK, GDN_DV, CONV_W = 6, 128, 128, 4
RW_HEADS, RW_N = 12, 64
RW_DECAY_LORA, RW_AAA_LORA, RW_MV_LORA, RW_GATE_LORA = 64, 64, 32, 128
RW_GN_EPS = 64e-5

GLA_QK = GLA_HEADS * GLA_DK
GLA_V = GLA_HEADS * GLA_DV
GDN_QK = GDN_HEADS * GDN_DK
GDN_V = GDN_HEADS * GDN_DV
GDN_CONV_CH = 2 * GDN_QK + GDN_V
RW_C = RW_HEADS * RW_N
GLA_PROJ = 2 * GLA_QK + GLA_V + GLA_LORA + GLA_V
GDN_PROJ = GDN_CONV_CH + 2 * GDN_HEADS + GDN_V
RW_PROJ = 3 * RW_C + RW_DECAY_LORA + RW_AAA_LORA + RW_GATE_LORA
D_FF = -(-8 * D_MODEL // (3 * 256)) * 256

LANES = 128
ZG_W = 2 * GLA_QK + 2 * GLA_V
ZD_W = GDN_CONV_CH + GDN_V
ZS_W = LANES
ZS_BETA = GLA_LORA
ZS_DT = GLA_LORA + GDN_HEADS
VMEM_LIMIT = 52 * 1024 * 1024

_BF = jnp.bfloat16
_F32 = jnp.float32
_NEG = -1e30


def _dot(a, b):
    return jnp.dot(a.astype(_BF), b.astype(_BF), preferred_element_type=_F32)


def _dot_nt(a, b):
    return lax.dot_general(a.astype(_BF), b.astype(_BF), (((1,), (1,)), ((), ())),
                           preferred_element_type=_F32)


def _dot_tn(a, b):
    return lax.dot_general(a.astype(_BF), b.astype(_BF), (((0,), (0,)), ((), ())),
                           preferred_element_type=_F32)


def _split2(x):
    hi = x.astype(_BF)
    lo = (x - hi.astype(_F32)).astype(_BF)
    return hi, lo


def _cdot(c, x):
    return sum(jnp.dot(c, p, preferred_element_type=_F32) for p in _split2(x))


def _cdot_tn(x, c):
    return sum(lax.dot_general(p, c, (((0,), (0,)), ((), ())), preferred_element_type=_F32)
               for p in _split2(x))


def _dot_hi(a, b):
    ah, al = _split2(a)
    bh, bl = _split2(b)
    d = functools.partial(jnp.dot, preferred_element_type=_F32)
    return d(ah, bh) + d(ah, bl) + d(al, bh)


def _dot_inv(a, b):
    return _dot(a, b)


def _sigmoid(x):
    return 1.0 / (1.0 + jnp.exp(-x))


def _silu(x):
    return x * _sigmoid(x)


def _softplus(x):
    return jnp.maximum(x, 0.0) + jnp.log(1.0 + jnp.exp(-jnp.abs(x)))


def _unit_lower_inverses(lows, n):
    rows = lax.broadcasted_iota(jnp.int32, (n, n), 0)
    cols = lax.broadcasted_iota(jnp.int32, (n, n), 1)
    eye = jnp.where(rows == cols, 1.0, 0.0)
    ps = [-low for low in lows]
    ts = [eye + p for p in ps]
    k = 2
    while k < n:
        ps = [_dot_inv(p, p) for p in ps]
        ts = [t + _dot_inv(t, p) for t, p in zip(ts, ps)]
        k *= 2
    return ts


def _lane_halves(shape):
    lane = lax.broadcasted_iota(jnp.int32, shape, len(shape) - 1)
    return lane < (LANES // 2)


def _half_sum(x, half):
    s0 = jnp.sum(jnp.where(half, x, 0.0), axis=-1, keepdims=True)
    s1 = jnp.sum(jnp.where(half, 0.0, x), axis=-1, keepdims=True)
    return jnp.where(half, s0, s1)


def _rmsnorm_rows(x, g):
    return x * lax.rsqrt(jnp.mean(x * x, axis=-1, keepdims=True) + NORM_EPS) * g


def _params(*sem):
    return pltpu.CompilerParams(dimension_semantics=sem, vmem_limit_bytes=VMEM_LIMIT)


def _norm_kernel(x_ref, g_ref, o_ref):
    o_ref[...] = _rmsnorm_rows(x_ref[...], g_ref[...]).astype(o_ref.dtype)


def _norm_call(x, g, tm):
    m, d = x.shape
    return pl.pallas_call(
        _norm_kernel,
        grid=(m // tm,),
        in_specs=[pl.BlockSpec((tm, d), lambda i: (i, 0)), pl.BlockSpec((1, d), lambda i: (0, 0))],
        out_specs=pl.BlockSpec((tm, d), lambda i: (i, 0)),
        out_shape=jax.ShapeDtypeStruct((m, d), _BF),
        compiler_params=_params("arbitrary"),
        name="rmsnorm",
    )(x, g)


def _mm_kernel(a_ref, w_ref, o_ref):
    o_ref[...] = jnp.dot(a_ref[...], w_ref[...], preferred_element_type=_F32)


def _mm_call(a, w, tm, tn, name):
    m, k = a.shape
    n = w.shape[1]
    return pl.pallas_call(
        _mm_kernel,
        grid=(n // tn, m // tm),
        in_specs=[pl.BlockSpec((tm, k), lambda j, i: (i, 0)), pl.BlockSpec((k, tn), lambda j, i: (0, j))],
        out_specs=pl.BlockSpec((tm, tn), lambda j, i: (i, j)),
        out_shape=jax.ShapeDtypeStruct((m, n), _F32),
        compiler_params=_params("arbitrary", "arbitrary"),
        name=name,
    )(a, w)


def _outproj_kernel(og_ref, od_ref, or_ref, w_ref, x_ref, g_ref, x1_ref, h_ref):
    d = functools.partial(jnp.dot, preferred_element_type=_F32)
    acc = (d(og_ref[...], w_ref[0:GLA_V, :]) + d(od_ref[...], w_ref[GLA_V:GLA_V + GDN_V, :])
           + d(or_ref[...], w_ref[GLA_V + GDN_V:, :]))
    x1 = x_ref[...] + acc
    x1_ref[...] = x1
    h_ref[...] = _rmsnorm_rows(x1, g_ref[...]).astype(h_ref.dtype)


def _outproj_call(og, od, orw, w, x, g, tm):
    m, d = x.shape
    row = lambda width: pl.BlockSpec((tm, width), lambda i: (i, 0))
    return pl.pallas_call(
        _outproj_kernel,
        grid=(m // tm,),
        in_specs=[row(GLA_V), row(GDN_V), row(RW_C), pl.BlockSpec(w.shape, lambda i: (0, 0)), row(d),
                  pl.BlockSpec((1, d), lambda i: (0, 0))],
        out_specs=[row(d), row(d)],
        out_shape=[jax.ShapeDtypeStruct((m, d), _F32), jax.ShapeDtypeStruct((m, d), _BF)],
        compiler_params=_params("arbitrary"),
        name="outproj",
    )(og, od, orw, w, x, g)


def _ffn_kernel(h_ref, wg_ref, wu_ref, wd_ref, x_ref, g_ref, *rest, nf, emit_x):
    if emit_x:
        x2_ref, hn_ref, acc_ref = rest
    else:
        hn_ref, acc_ref = rest
    f = pl.program_id(1)

    @pl.when(f == 0)
    def _():
        acc_ref[...] = jnp.zeros_like(acc_ref)

    h = h_ref[...]
    gate = jnp.dot(h, wg_ref[...], preferred_element_type=_F32)
    up = jnp.dot(h, wu_ref[...], preferred_element_type=_F32)
    act = (_silu(gate) * up).astype(_BF)
    acc_ref[...] += jnp.dot(act, wd_ref[...], preferred_element_type=_F32)

    @pl.when(f == nf - 1)
    def _():
        x2 = x_ref[...] + acc_ref[...]
        if emit_x:
            x2_ref[...] = x2
        hn_ref[...] = _rmsnorm_rows(x2, g_ref[...]).astype(hn_ref.dtype)


def _ffn_call(h, wg, wu, wd, x, g, tm, tf, last):
    m, d = x.shape
    nf = D_FF // tf
    row = pl.BlockSpec((tm, d), lambda i, f: (i, 0))
    if last:
        out_specs = [row]
        out_shape = [jax.ShapeDtypeStruct((m, d), _F32)]
    else:
        out_specs = [row, row]
        out_shape = [jax.ShapeDtypeStruct((m, d), _F32), jax.ShapeDtypeStruct((m, d), _BF)]
    return pl.pallas_call(
        functools.partial(_ffn_kernel, nf=nf, emit_x=not last),
        grid=(m // tm, nf),
        in_specs=[row, pl.BlockSpec((d, tf), lambda i, f: (0, f)), pl.BlockSpec((d, tf), lambda i, f: (0, f)),
                  pl.BlockSpec((tf, d), lambda i, f: (f, 0)), row, pl.BlockSpec((1, d), lambda i, f: (0, 0))],
        out_specs=out_specs,
        out_shape=out_shape,
        scratch_shapes=[pltpu.VMEM((tm, d), _F32)],
        compiler_params=_params("arbitrary", "arbitrary"),
        name="ffn",
    )(h, wg, wu, wd, x, g)


def _gla_consts(c):
    nlev = int(np.log2(c))
    t = np.arange(c)
    blocks = [t[:, None] >= t[None, :],
              t[None, :] > t[:, None]]
    mq, mk, masks = [], [], []
    for lv in range(nlev):
        m = c >> lv
        half = m // 2
        blk, pos = t // m, t % m
        mid = blk * m + half
        upper = pos >= half
        mq.append(upper[:, None] & (t[None, :] >= mid[:, None]) & (t[None, :] <= t[:, None]))
        mk.append((~upper)[:, None] & (t[None, :] > t[:, None]) & (t[None, :] <= mid[:, None] - 1))
        masks.append((blk[:, None] == blk[None, :]) & upper[:, None] & (~upper)[None, :])
    mc = np.concatenate(blocks + mq + mk, axis=0).astype(np.float32)
    return jnp.asarray(mc, _BF), jnp.asarray(np.stack(masks).astype(np.float32)), nlev


def _gla_kernel(zg_ref, zs_ref, aup_ref, abias_ref, gn_ref, s0_ref, mc_ref, mk_ref,
                o_ref, sout_ref, s_scr, *, c, nlev, nc):
    step = pl.program_id(1)

    @pl.when(step == 0)
    def _():
        s_scr[...] = s0_ref[0]

    xa = _dot(zs_ref[...], aup_ref[...]) + abias_ref[...]
    la = (jnp.minimum(xa, 0.0) - jnp.log(1.0 + jnp.exp(-jnp.abs(xa)))) * (1.0 / GLA_GATE_TEMP)
    ex = jnp.exp(_cdot(mc_ref[...], la))
    q = zg_ref[:, 0:GLA_QK] * (GLA_DK ** -0.5)
    k = zg_ref[:, GLA_QK:2 * GLA_QK]
    qd = q * ex[0:c]
    kd = k * ex[c:2 * c]
    half = _lane_halves((c, LANES))
    eye = lax.broadcasted_iota(jnp.int32, (c, c), 0) == lax.broadcasted_iota(jnp.int32, (c, c), 1)
    ones = jnp.ones((c, LANES), _BF)
    heads = range(GLA_HEADS)
    sls = [slice(LANES * (h // 2), LANES * (h // 2 + 1)) for h in heads]
    sel = [(lambda x: jnp.where(half, x, 0.0)) if h % 2 == 0 else (lambda x: jnp.where(half, 0.0, x)) for h in heads]
    v = [zg_ref[:, 2 * GLA_QK + GLA_DV * h:2 * GLA_QK + GLA_DV * (h + 1)] for h in heads]
    s_old = [s_scr[p] for p in range(GLA_HEADS // 2)]
    a = [jnp.where(eye, _dot_nt(sel[h](q[:, sls[h]]), k[:, sls[h]]), 0.0) for h in heads]
    for lv in range(nlev):
        exq = ex[(2 + lv) * c:(3 + lv) * c]
        kx = k * ex[(2 + nlev + lv) * c:(3 + nlev + lv) * c]
        qx = q * exq
        a = [a[h] + mk_ref[lv] * _dot_nt(sel[h](qx[:, sls[h]]), kx[:, sls[h]]) for h in heads]
    o = [_dot(sel[h](qd[:, sls[h]]), s_old[h // 2]) + _dot(a[h], v[h]) for h in heads]
    for p in range(GLA_HEADS // 2):
        sl = sls[2 * p]
        s_scr[p] = (jnp.exp(_cdot_tn(la[:, sl], ones)) * s_old[p]
                    + _dot_tn(sel[0](kd[:, sl]), v[2 * p]) + _dot_tn(sel[1](kd[:, sl]), v[2 * p + 1]))
    for h in heads:
        g_h = zg_ref[:, 2 * GLA_QK + GLA_V + GLA_DV * h:2 * GLA_QK + GLA_V + GLA_DV * (h + 1)]
        y = _rmsnorm_rows(o[h], gn_ref[...]) * _silu(g_h)
        o_ref[:, GLA_DV * h:GLA_DV * (h + 1)] = y.astype(o_ref.dtype)

    @pl.when(step == nc - 1)
    def _():
        sout_ref[0] = s_scr[...]


def _gla_call(zg, zs, aup, abias, gn, s0, b, t):
    c = min(CHUNK, t)
    nc = t // c
    mc, masks, nlev = _gla_consts(c)
    full = lambda a: pl.BlockSpec(a.shape, lambda i, j: (0,) * a.ndim)
    st = pl.BlockSpec((1,) + s0.shape[1:], lambda i, j: (i, 0, 0, 0))
    tok = lambda w: pl.BlockSpec((c, w), lambda i, j: (i * nc + j, 0))
    return pl.pallas_call(
        functools.partial(_gla_kernel, c=c, nlev=nlev, nc=nc),
        grid=(b, nc),
        in_specs=[tok(ZG_W), tok(ZS_W), full(aup), full(abias), full(gn), st, full(mc), full(masks)],
        out_specs=[tok(GLA_V), st],
        out_shape=[jax.ShapeDtypeStruct((b * t, GLA_V), _BF), jax.ShapeDtypeStruct(s0.shape, _F32)],
        scratch_shapes=[pltpu.VMEM(s0.shape[1:], _F32)],
        compiler_params=_params("arbitrary", "arbitrary"),
        name="gla",
    )(zg, zs, aup, abias, gn, s0, mc, masks)


def _gdn_kernel(zd_ref, zs_ref, cw_ref, alog_ref, dtb_ref, gn_ref, cache_ref, s0_ref, tril_ref, triu_ref, mlast_ref,
                o_ref, cout_ref, sout_ref, cbuf, s_scr, *, c, nc):
    step = pl.program_id(1)
    tail0 = 8 - (CONV_W - 1)

    @pl.when(step == 0)
    def _():
        cbuf[tail0:8, :] = cache_ref[0]
        s_scr[...] = s0_ref[0]

    x = zd_ref[:, 0:GDN_CONV_CH]
    cbuf[8:8 + c, :] = x
    y = cbuf[tail0:tail0 + c, :] * cw_ref[0:1, :]
    for j in range(1, CONV_W - 1):
        y = y + cbuf[tail0 + j:tail0 + j + c, :] * cw_ref[j:j + 1, :]
    y = y + x * cw_ref[CONV_W - 1:CONV_W, :]
    tail = cbuf[c + tail0:c + 8, :]
    cbuf[tail0:8, :] = tail

    @pl.when(step == nc - 1)
    def _():
        cout_ref[0] = tail

    ys = _silu(y)
    zs = zs_ref[...]
    lg = -jnp.exp(alog_ref[...]) * _softplus(zs + dtb_ref[...])
    bt = _sigmoid(zs)
    bcum = _cdot(tril_ref[...], lg)
    bcum_t = _cdot_tn(lg, triu_ref[...])
    bdl = _cdot(mlast_ref[...], lg)
    btot_t = _cdot_tn(lg, jnp.ones((c, LANES), _BF))
    lane = lax.broadcasted_iota(jnp.int32, (c, LANES), 1)
    rows = lax.broadcasted_iota(jnp.int32, (c, c), 0)
    cols = lax.broadcasted_iota(jnp.int32, (c, c), 1)
    heads = range(GDN_HEADS)
    pick = lambda tile, j: jnp.sum(jnp.where(lane == j, tile, 0.0), axis=1, keepdims=True)
    bcol = [pick(bcum, ZS_DT + h) for h in heads]
    beta = [pick(bt, ZS_BETA + h) for h in heads]
    dlcol = [pick(bdl, ZS_DT + h) for h in heads]
    dec = [jnp.exp(jnp.where(rows >= cols, bcol[h] - bcum_t[ZS_DT + h:ZS_DT + h + 1, :], _NEG)) for h in heads]
    eb = [jnp.exp(bcol[h]) for h in heads]
    q, k, v = [], [], []
    for h in heads:
        q_h = ys[:, GDN_DK * h:GDN_DK * (h + 1)]
        k_h = ys[:, GDN_QK + GDN_DK * h:GDN_QK + GDN_DK * (h + 1)]
        q.append(q_h * lax.rsqrt(jnp.sum(q_h * q_h, axis=-1, keepdims=True) + L2_EPS) * (GDN_DK ** -0.5))
        k.append(k_h * lax.rsqrt(jnp.sum(k_h * k_h, axis=-1, keepdims=True) + L2_EPS))
        v.append(ys[:, 2 * GDN_QK + GDN_DV * h:2 * GDN_QK + GDN_DV * (h + 1)])
    qk = [jnp.concatenate([q[h], k[h]], axis=0) for h in heads]
    s_old = [s_scr[h] for h in heads]
    sc = [_dot_nt(qk[h], k[h]) for h in heads]
    ps = [_dot(qk[h], s_old[h]) for h in heads]
    tinv = _unit_lower_inverses([jnp.where(rows > cols, beta[h] * dec[h] * sc[h][c:], 0.0) for h in heads], c)
    delta = [_dot_hi(tinv[h], beta[h] * (v[h] - eb[h] * ps[h][c:])) for h in heads]
    o = [eb[h] * ps[h][:c] + _dot(sc[h][:c] * dec[h], delta[h]) for h in heads]
    for h in heads:
        elast = jnp.exp(btot_t[ZS_DT + h:ZS_DT + h + 1, :])
        s_scr[h] = elast * s_old[h] + _dot_tn(k[h] * jnp.exp(dlcol[h]), delta[h])
    for h in heads:
        g_h = zd_ref[:, GDN_CONV_CH + GDN_DV * h:GDN_CONV_CH + GDN_DV * (h + 1)]
        y_h = _rmsnorm_rows(o[h], gn_ref[...]) * _silu(g_h)
        o_ref[:, GDN_DV * h:GDN_DV * (h + 1)] = y_h.astype(o_ref.dtype)

    @pl.when(step == nc - 1)
    def _():
        sout_ref[0] = s_scr[...]


def _tri_consts(c):
    t = np.arange(c)
    tril = (t[:, None] >= t[None, :]).astype(np.float32)
    mlast = (t[None, :] > t[:, None]).astype(np.float32)
    return jnp.asarray(tril, _BF), jnp.asarray(tril.T, _BF), jnp.asarray(mlast, _BF)


def _gdn_call(zd, zs, cw, alog, dtb, gn, cache, s0, b, t):
    c = min(CHUNK, t)
    nc = t // c
    tril, triu, mlast = _tri_consts(c)
    full = lambda a: pl.BlockSpec(a.shape, lambda i, j: (0,) * a.ndim)
    st = pl.BlockSpec((1,) + s0.shape[1:], lambda i, j: (i, 0, 0, 0))
    ch = pl.BlockSpec((1,) + cache.shape[1:], lambda i, j: (i, 0, 0))
    tok = lambda w: pl.BlockSpec((c, w), lambda i, j: (i * nc + j, 0))
    return pl.pallas_call(
        functools.partial(_gdn_kernel, c=c, nc=nc),
        grid=(b, nc),
        in_specs=[tok(ZD_W), tok(ZS_W), full(cw), full(alog), full(dtb), full(gn), ch, st, full(tril), full(triu),
                  full(mlast)],
        out_specs=[tok(GDN_V), ch, st],
        out_shape=[jax.ShapeDtypeStruct((b * t, GDN_V), _BF), jax.ShapeDtypeStruct(cache.shape, _F32),
                   jax.ShapeDtypeStruct(s0.shape, _F32)],
        scratch_shapes=[pltpu.VMEM((c + 8, GDN_CONV_CH), _F32), pltpu.VMEM(s0.shape[1:], _F32)],
        compiler_params=_params("arbitrary", "arbitrary"),
        name="gdn",
    )(zd, zs, cw, alog, dtb, gn, cache, s0, tril, triu, mlast)


def _rw_kernel(*refs, c, nc, first):
    if first:
        (zr_ref, mu_ref, w0_ref, wup_ref, a0_ref, aup_ref, gup_ref, kk_ref, ka_ref, rk_ref, lng_ref, lnb_ref,
         cache_ref, s0_ref, tril_ref, mlast_ref, o_ref, vf_out_ref, shout_ref, sout_ref, sbuf, s_scr) = refs
    else:
        (zr_ref, vf_ref, v0_ref, vdown_ref, vup_ref, mu_ref, w0_ref, wup_ref, a0_ref, aup_ref, gup_ref, kk_ref,
         ka_ref, rk_ref, lng_ref, lnb_ref, cache_ref, s0_ref, tril_ref, mlast_ref, o_ref, shout_ref, sout_ref,
         sbuf, s_scr) = refs
    step = pl.program_id(1)

    @pl.when(step == 0)
    def _():
        sbuf[7:8, :] = cache_ref[0]
        s_scr[...] = s0_ref[0]

    x = zr_ref[...]
    sbuf[8:8 + c, :] = x
    zprev = sbuf[7:7 + c, :]
    last = sbuf[c + 7:c + 8, :]
    sbuf[7:8, :] = last

    @pl.when(step == nc - 1)
    def _():
        shout_ref[0] = last

    zm = x + (zprev - x) * mu_ref[...]
    xr, xk, xv = zm[:, 0:RW_C], zm[:, RW_C:2 * RW_C], zm[:, 2 * RW_C:3 * RW_C]
    xwa = zm[:, 3 * RW_C:3 * RW_C + LANES]
    xg = zm[:, 3 * RW_C + LANES:]
    wlog = -_softplus(-(w0_ref[...] + _dot(jnp.tanh(xwa), wup_ref[...]))) - 0.5
    lw = -jnp.exp(wlog)
    a = _sigmoid(a0_ref[...] + _dot(xwa, aup_ref[...]))
    if first:
        vf_out_ref[...] = xv
    else:
        nu = _sigmoid(v0_ref[...] + _dot(_dot(xv, vdown_ref[...]), vup_ref[...]))
        xv = xv + (vf_ref[...] - xv) * nu
    kkp = xk * kk_ref[...]
    xk2 = xk * (1.0 + (a - 1.0) * ka_ref[...])
    gate = _dot(_sigmoid(xg), gup_ref[...])
    cum = _cdot(tril_ref[...], lw)
    e_c = jnp.exp(cum)
    e_cp = jnp.exp(cum - lw)
    e_nc = jnp.exp(-cum)
    e_dl = jnp.exp(_cdot(mlast_ref[...], lw))
    half = _lane_halves((c, LANES))
    half2 = _lane_halves((2 * c, LANES))
    rows = lax.broadcasted_iota(jnp.int32, (c, c), 0)
    cols = lax.broadcasted_iota(jnp.int32, (c, c), 1)
    srow = lax.broadcasted_iota(jnp.int32, (LANES, LANES), 0) < (LANES // 2)
    scol = lax.broadcasted_iota(jnp.int32, (LANES, LANES), 1) < (LANES // 2)
    same_head = srow == scol
    ones = jnp.ones((c, LANES), _BF)
    pairs = range(RW_HEADS // 2)
    sls = [slice(LANES * p, LANES * (p + 1)) for p in pairs]
    r, k, v = [xr[:, s] for s in sls], [xk2[:, s] for s in sls], [xv[:, s] for s in sls]
    kap = [kkp[:, s] for s in sls]
    kap = [x * lax.rsqrt(_half_sum(x * x, half) + L2_EPS) for x in kap]
    ahat = [-(kap[p] * a[:, sls[p]]) for p in pairs]
    x2 = [jnp.concatenate([kap[p] * e_cp[:, sls[p]], r[p] * e_c[:, sls[p]]], axis=0) for p in pairs]
    at = [ahat[p] * e_nc[:, sls[p]] for p in pairs]
    kt = [k[p] * e_nc[:, sls[p]] for p in pairs]
    s_bd = [s_scr[p] for p in pairs]
    ps = [_dot(x2[p], s_bd[p]) for p in pairs]
    aa, ak = [], []
    for p in pairs:
        for xm in (jnp.where(half2, x2[p], 0.0), jnp.where(half2, 0.0, x2[p])):
            aa.append(_dot_nt(xm, at[p]))
            ak.append(_dot_nt(xm, kt[p]))
    tinv = _unit_lower_inverses([jnp.where(rows > cols, -x[:c], 0.0) for x in aa], c)
    akv = [_dot(jnp.where(rows > cols, ak[i][:c], 0.0), v[i // 2]) for i in range(RW_HEADS)]
    rhs = [ps[p][:c] + jnp.where(half, akv[2 * p], akv[2 * p + 1]) for p in pairs]
    u = [jnp.where(half, _dot_hi(tinv[2 * p], rhs[p]), _dot_hi(tinv[2 * p + 1], rhs[p])) for p in pairs]
    oh = [_dot(jnp.where(rows >= cols, aa[i][c:], 0.0), u[i // 2])
          + _dot(jnp.where(rows >= cols, ak[i][c:], 0.0), v[i // 2]) for i in range(RW_HEADS)]
    o = [ps[p][c:] + jnp.where(half, oh[2 * p], oh[2 * p + 1]) for p in pairs]
    for p in pairs:
        upd = _dot_tn(ahat[p] * e_dl[:, sls[p]], u[p]) + _dot_tn(k[p] * e_dl[:, sls[p]], v[p])
        s_scr[p] = jnp.exp(_cdot_tn(lw[:, sls[p]], ones)) * s_bd[p] + jnp.where(same_head, upd, 0.0)
    for p in pairs:
        mean = _half_sum(o[p], half) * (1.0 / RW_N)
        dev = o[p] - mean
        var = _half_sum(dev * dev, half) * (1.0 / RW_N)
        on = dev * lax.rsqrt(var + RW_GN_EPS) * lng_ref[:, sls[p]] + lnb_ref[:, sls[p]]
        bonus = _half_sum(r[p] * k[p] * rk_ref[:, sls[p]], half) * v[p]
        o_ref[:, sls[p]] = ((on + bonus) * gate[:, sls[p]]).astype(o_ref.dtype)

    @pl.when(step == nc - 1)
    def _():
        sout_ref[0] = s_scr[...]


def _rw_call(zr, vf, vmix, prm, cache, s0, b, t):
    c = min(CHUNK, t)
    nc = t // c
    tril, _, mlast = _tri_consts(c)
    first = vf is None
    full = lambda a: pl.BlockSpec(a.shape, lambda i, j: (0,) * a.ndim)
    st = pl.BlockSpec((1,) + s0.shape[1:], lambda i, j: (i, 0, 0, 0))
    ch = pl.BlockSpec((1,) + cache.shape[1:], lambda i, j: (i, 0, 0))
    tok = lambda w: pl.BlockSpec((c, w), lambda i, j: (i * nc + j, 0))
    m = b * t
    args = [zr] + ([] if first else [vf] + list(vmix)) + list(prm) + [cache, s0, tril, mlast]
    in_specs = ([tok(RW_PROJ)] + ([] if first else [tok(RW_C)] + [full(a) for a in vmix])
                + [full(a) for a in prm] + [ch, st, full(tril), full(mlast)])
    out_specs = [tok(RW_C)] + ([tok(RW_C)] if first else []) + [ch, st]
    out_shape = ([jax.ShapeDtypeStruct((m, RW_C), _BF)] + ([jax.ShapeDtypeStruct((m, RW_C), _F32)] if first else [])
                 + [jax.ShapeDtypeStruct(cache.shape, _F32), jax.ShapeDtypeStruct(s0.shape, _F32)])
    return pl.pallas_call(
        functools.partial(_rw_kernel, c=c, nc=nc, first=first),
        grid=(b, nc),
        in_specs=in_specs,
        out_specs=out_specs,
        out_shape=out_shape,
        scratch_shapes=[pltpu.VMEM((c + 8, RW_PROJ), _F32), pltpu.VMEM(s0.shape[1:], _F32)],
        compiler_params=_params("arbitrary", "arbitrary"),
        name="rwkv",
    )(*args)


def _prep_weights(p):
    w_in = p["w_in"]
    o1 = GLA_PROJ
    o2 = GLA_PROJ + GDN_PROJ
    lora0 = 2 * GLA_QK + GLA_V
    w_gla = jnp.concatenate([w_in[:, :, 0:lora0], w_in[:, :, lora0 + GLA_LORA:o1]], axis=2)
    w_gdn = jnp.concatenate([w_in[:, :, o1:o1 + GDN_CONV_CH], w_in[:, :, o1 + GDN_CONV_CH + 2 * GDN_HEADS:o2]], axis=2)
    w_rw = w_in[:, :, o2:]
    narrow = jnp.concatenate([w_in[:, :, lora0:lora0 + GLA_LORA],
                              w_in[:, :, o1 + GDN_CONV_CH:o1 + GDN_CONV_CH + 2 * GDN_HEADS]], axis=2)
    w_small = jnp.pad(narrow, ((0, 0), (0, 0), (0, ZS_W - narrow.shape[2])))
    bf = lambda a: a.astype(_BF)
    pad_rows = lambda a, lo, hi: jnp.pad(a, ((0, 0), (lo, hi), (0, 0)))
    pad_lanes = lambda a, lo: jnp.pad(a, ((0, 0), (lo, ZS_W - lo - a.shape[1])))[:, None, :]
    row = lambda a: a.reshape(a.shape[0], 1, -1)
    return dict(
        w_gla=bf(w_gla), w_gdn=bf(w_gdn), w_rw=bf(w_rw), w_small=bf(w_small),
        w_out=bf(p["w_out"]), w_g=bf(p["w_ffn_gate"]), w_u=bf(p["w_ffn_up"]), w_d=bf(p["w_ffn_down"]),
        gla_aup=bf(pad_rows(p["gla_a_up"], 0, ZS_W - GLA_LORA)), gla_abias=row(p["gla_a_bias"]),
        gla_gn=row(p["gla_norm_g"]),
        gdn_cw=p["gdn_conv_w"], gdn_alog=pad_lanes(p["gdn_A_log"], ZS_DT), gdn_dtb=pad_lanes(p["gdn_dt_bias"], ZS_DT),
        gdn_gn=row(p["gdn_norm_g"]),
        rw_mu=row(p["rw_mu"]), rw_w0=row(p["rw_w0"]), rw_wup=bf(pad_rows(p["rw_w_up"], 0, LANES - RW_DECAY_LORA)),
        rw_a0=row(p["rw_a0"]), rw_aup=bf(pad_rows(p["rw_a_up"], RW_DECAY_LORA, 0)), rw_gup=bf(p["rw_g_up"]),
        rw_kk=row(p["rw_k_k"]), rw_ka=row(p["rw_k_a"]), rw_rk=p["rw_r_k"].reshape(DEPTH, 1, RW_C),
        rw_lng=row(p["rw_ln_g"]), rw_lnb=row(p["rw_ln_b"]),
        rw_v0=row(p["rw_v0"]), rw_vdown=bf(p["rw_v_down"]), rw_vup=bf(p["rw_v_up"]),
        norm1=row(p["norm1_g"]), norm2=row(p["norm2_g"]), final=p["final_norm_g"].reshape(1, -1),
    )


def _pair_block_diag(s):
    b, h, n, _ = s.shape
    s = s.reshape(b, h // 2, 2, n, n)
    z = jnp.zeros_like(s[:, :, 0])
    top = jnp.concatenate([s[:, :, 0], z], axis=-1)
    bot = jnp.concatenate([z, s[:, :, 1]], axis=-1)
    return jnp.concatenate([top, bot], axis=-2)


def _pair_blocks(s):
    b, hp, n2, _ = s.shape
    n = n2 // 2
    return jnp.stack([s[:, :, :n, :n], s[:, :, n:, n:]], axis=2).reshape(b, 2 * hp, n, n)


def _trunk(x, s_gla, s_gdn, c_gdn, s_rw, c_rw, w):
    b, t, d = x.shape
    m = b * t
    tm = min(512, m)
    xf = x.reshape(m, d)
    h = _norm_call(xf, w["norm1"][0], tm)
    outs = ([], [], [], [], [])
    vf = None
    for l in range(DEPTH):
        zg = _mm_call(h, w["w_gla"][l], tm, ZG_W // 2, "proj_gla")
        zd = _mm_call(h, w["w_gdn"][l], tm, ZD_W // 2, "proj_gdn")
        zr = _mm_call(h, w["w_rw"][l], tm, RW_PROJ // 2, "proj_rw")
        zs = _mm_call(h, w["w_small"][l], tm, ZS_W, "proj_small")
        og, sg = _gla_call(zg, zs, w["gla_aup"][l], w["gla_abias"][l], w["gla_gn"][l],
                           s_gla[l].reshape(b, GLA_HEADS // 2, 2 * GLA_DK, GLA_DV), b, t)
        od, cg, sd = _gdn_call(zd, zs, w["gdn_cw"][l], w["gdn_alog"][l], w["gdn_dtb"][l], w["gdn_gn"][l],
                               c_gdn[l], s_gdn[l], b, t)
        prm = [w[n][l] for n in ("rw_mu", "rw_w0", "rw_wup", "rw_a0", "rw_aup", "rw_gup", "rw_kk", "rw_ka",
                                 "rw_rk", "rw_lng", "rw_lnb")]
        vmix = None if l == 0 else [w[n][l - 1] for n in ("rw_v0", "rw_vdown", "rw_vup")]
        res = _rw_call(zr, vf, vmix, prm, c_rw[l], _pair_block_diag(s_rw[l]), b, t)
        if l == 0:
            orw, vf, cr, sr = res
        else:
            orw, cr, sr = res
        xf, h2 = _outproj_call(og, od, orw, w["w_out"][l], xf, w["norm2"][l], min(256, m))
        last = l == DEPTH - 1
        g_next = w["final"] if last else w["norm1"][l + 1]
        res = _ffn_call(h2, w["w_g"][l], w["w_u"][l], w["w_d"][l], xf, g_next, tm, 512, last)
        if last:
            (y,) = res
        else:
            xf, h = res
        for lst, arr in zip(outs, (sg.reshape(b, GLA_HEADS, GLA_DK, GLA_DV), sd, cg, _pair_blocks(sr), cr)):
            lst.append(arr)
    return y.reshape(b, t, d), [jnp.stack(lst) for lst in outs]


def kernel(x_prompt, x_sample, state_gla, state_gdn, cache_gdn_conv, state_rwkv, cache_rwkv_shift, norm1_g, w_in, gla_a_up, gla_a_bias, gla_norm_g, gdn_conv_w, gdn_A_log, gdn_dt_bias, gdn_norm_g, rw_mu, rw_w0, rw_w_up, rw_a0, rw_a_up, rw_v0, rw_v_down, rw_v_up, rw_g_up, rw_k_k, rw_k_a, rw_r_k, rw_ln_g, rw_ln_b, w_out, norm2_g, w_ffn_gate, w_ffn_up, w_ffn_down, final_norm_g):
    p = dict(norm1_g=norm1_g, w_in=w_in, gla_a_up=gla_a_up, gla_a_bias=gla_a_bias, gla_norm_g=gla_norm_g,
             gdn_conv_w=gdn_conv_w, gdn_A_log=gdn_A_log, gdn_dt_bias=gdn_dt_bias, gdn_norm_g=gdn_norm_g,
             rw_mu=rw_mu, rw_w0=rw_w0, rw_w_up=rw_w_up, rw_a0=rw_a0, rw_a_up=rw_a_up, rw_v0=rw_v0,
             rw_v_down=rw_v_down, rw_v_up=rw_v_up, rw_g_up=rw_g_up, rw_k_k=rw_k_k, rw_k_a=rw_k_a,
             rw_r_k=rw_r_k, rw_ln_g=rw_ln_g, rw_ln_b=rw_ln_b, w_out=w_out, norm2_g=norm2_g,
             w_ffn_gate=w_ffn_gate, w_ffn_up=w_ffn_up, w_ffn_down=w_ffn_down, final_norm_g=final_norm_g)
    w = _prep_weights(p)
    bp = x_prompt.shape[0]
    zeros = lambda a: jnp.zeros((DEPTH, bp) + a.shape[2:], a.dtype)
    y_p, st_p = _trunk(x_prompt, zeros(state_gla), zeros(state_gdn), zeros(cache_gdn_conv), zeros(state_rwkv),
                       zeros(cache_rwkv_shift), w)
    y_s, st_s = _trunk(x_sample, state_gla, state_gdn, cache_gdn_conv, state_rwkv, cache_rwkv_shift, w)
    return (y_p, y_s, *st_p, *st_s)
```

```python
import functools

import numpy as np
import jax
import jax.numpy as jnp
from jax import lax
from jax.experimental import pallas as pl
from jax.experimental.pallas import tpu as pltpu

D_MODEL = 2048
DEPTH = 4
CHUNK = 64
NORM_EPS = 1e-6
L2_EPS = 1e-6
GLA_HEADS, GLA_DK, GLA_DV, GLA_LORA, GLA_GATE_TEMP = 4, 64, 128, 16, 16.0
GDN_HEADS, GDN_DK, GDN_DV, CONV_W = 6, 128, 128, 4
RW_HEADS, RW_N = 12, 64
RW_DECAY_LORA, RW_AAA_LORA, RW_MV_LORA, RW_GATE_LORA = 64, 64, 32, 128
RW_GN_EPS = 64e-5

GLA_QK = GLA_HEADS * GLA_DK
GLA_V = GLA_HEADS * GLA_DV
GDN_QK = GDN_HEADS * GDN_DK
GDN_V = GDN_HEADS * GDN_DV
GDN_CONV_CH = 2 * GDN_QK + GDN_V
RW_C = RW_HEADS * RW_N
GLA_PROJ = 2 * GLA_QK + GLA_V + GLA_LORA + GLA_V
GDN_PROJ = GDN_CONV_CH + 2 * GDN_HEADS + GDN_V
RW_PROJ = 3 * RW_C + RW_DECAY_LORA + RW_AAA_LORA + RW_GATE_LORA
D_FF = -(-8 * D_MODEL // (3 * 256)) * 256

LANES = 128
ZG_W = 2 * GLA_QK + 2 * GLA_V
ZD_W = GDN_CONV_CH + GDN_V
ZS_W = LANES
ZS_BETA = GLA_LORA
ZS_DT = GLA_LORA + GDN_HEADS
VMEM_LIMIT = 52 * 1024 * 1024

_BF = jnp.bfloat16
_F32 = jnp.float32
_NEG = -1e30


def _dot(a, b):
    return jnp.dot(a.astype(_BF), b.astype(_BF), preferred_element_type=_F32)


def _dot_nt(a, b):
    return lax.dot_general(a.astype(_BF), b.astype(_BF), (((1,), (1,)), ((), ())),
                           preferred_element_type=_F32)


def _dot_tn(a, b):
    return lax.dot_general(a.astype(_BF), b.astype(_BF), (((0,), (0,)), ((), ())),
                           preferred_element_type=_F32)


def _split2(x):
    hi = x.astype(_BF)
    lo = (x - hi.astype(_F32)).astype(_BF)
    return hi, lo


def _cdot(c, x):
    return sum(jnp.dot(c, p, preferred_element_type=_F32) for p in _split2(x))


def _cdot_tn(x, c):
    return sum(lax.dot_general(p, c, (((0,), (0,)), ((), ())), preferred_element_type=_F32)
               for p in _split2(x))


def _sigmoid(x):
    return 1.0 / (1.0 + jnp.exp(-x))


def _silu(x):
    return x * _sigmoid(x)


def _softplus(x):
    return jnp.maximum(x, 0.0) + jnp.log(1.0 + jnp.exp(-jnp.abs(x)))


HALF = LANES // 2


def _rows_at(x, first, total=LANES):
    n, w = x.shape
    parts = [jnp.zeros((first, w), x.dtype)] if first else []
    parts.append(x)
    if total - first - n:
        parts.append(jnp.zeros((total - first - n, w), x.dtype))
    return jnp.concatenate(parts, axis=0) if len(parts) > 1 else x


def _rows_pair(a, b):
    return jnp.concatenate([_rows_at(a, 0, HALF), _rows_at(b, 0, HALF)], axis=0)


def _inverse_start(lows, n):
    rows = lax.broadcasted_iota(jnp.int32, (n, LANES), 0)
    lane = lax.broadcasted_iota(jnp.int32, (n, LANES), 1)
    return [jnp.where(lane == rows + HALF, 1.0, 0.0) - low for low in lows]


def _inverse_step(ws, n):
    keep = lax.broadcasted_iota(jnp.int32, (n, LANES), 1) >= HALF
    ys = [_dot(w, _rows_at(w, 0)) for w in ws]
    return [y + jnp.where(keep, w, 0.0) for y, w in zip(ys, ws)]


def _solve_with(w, rhs):
    hi, lo = _split2(_rows_at(rhs, HALF))
    wb = w.astype(_BF)
    return jnp.dot(wb, hi, preferred_element_type=_F32) + jnp.dot(wb, lo, preferred_element_type=_F32)


def _lane_halves(shape):
    lane = lax.broadcasted_iota(jnp.int32, shape, len(shape) - 1)
    return lane < (LANES // 2)


def _half_sum(x, half):
    s0 = jnp.sum(jnp.where(half, x, 0.0), axis=-1, keepdims=True)
    s1 = jnp.sum(jnp.where(half, 0.0, x), axis=-1, keepdims=True)
    return jnp.where(half, s0, s1)


def _rmsnorm_rows(x, g):
    return x * lax.rsqrt(jnp.mean(x * x, axis=-1, keepdims=True) + NORM_EPS) * g


def _params(*sem):
    return pltpu.CompilerParams(dimension_semantics=sem, vmem_limit_bytes=VMEM_LIMIT)


def _norm_kernel(x_ref, g_ref, o_ref):
    o_ref[...] = _rmsnorm_rows(x_ref[...], g_ref[...]).astype(o_ref.dtype)


def _norm_call(x, g, tm):
    m, d = x.shape
    return pl.pallas_call(
        _norm_kernel,
        grid=(m // tm,),
        in_specs=[pl.BlockSpec((tm, d), lambda i: (i, 0)), pl.BlockSpec((1, d), lambda i: (0, 0))],
        out_specs=pl.BlockSpec((tm, d), lambda i: (i, 0)),
        out_shape=jax.ShapeDtypeStruct((m, d), _BF),
        compiler_params=_params("arbitrary"),
        name="rmsnorm",
    )(x, g)


def _mm_kernel(a_ref, w_ref, o_ref):
    o_ref[...] = jnp.dot(a_ref[...], w_ref[...], preferred_element_type=_F32)


def _mm_call(a, w, tm, tn, name):
    m, k = a.shape
    n = w.shape[1]
    return pl.pallas_call(
        _mm_kernel,
        grid=(n // tn, m // tm),
        in_specs=[pl.BlockSpec((tm, k), lambda j, i: (i, 0)), pl.BlockSpec((k, tn), lambda j, i: (0, j))],
        out_specs=pl.BlockSpec((tm, tn), lambda j, i: (i, j)),
        out_shape=jax.ShapeDtypeStruct((m, n), _F32),
        compiler_params=_params("arbitrary", "arbitrary"),
        name=name,
    )(a, w)


def _outproj_kernel(og_ref, od_ref, or_ref, w_ref, x_ref, g_ref, x1_ref, h_ref):
    d = functools.partial(jnp.dot, preferred_element_type=_F32)
    acc = (d(og_ref[...], w_ref[0:GLA_V, :]) + d(od_ref[...], w_ref[GLA_V:GLA_V + GDN_V, :])
           + d(or_ref[...], w_ref[GLA_V + GDN_V:, :]))
    x1 = x_ref[...] + acc
    x1_ref[...] = x1
    h_ref[...] = _rmsnorm_rows(x1, g_ref[...]).astype(h_ref.dtype)


def _outproj_call(og, od, orw, w, x, g, tm):
    m, d = x.shape
    row = lambda width: pl.BlockSpec((tm, width), lambda i: (i, 0))
    return pl.pallas_call(
        _outproj_kernel,
        grid=(m // tm,),
        in_specs=[row(GLA_V), row(GDN_V), row(RW_C), pl.BlockSpec(w.shape, lambda i: (0, 0)), row(d),
                  pl.BlockSpec((1, d), lambda i: (0, 0))],
        out_specs=[row(d), row(d)],
        out_shape=[jax.ShapeDtypeStruct((m, d), _F32), jax.ShapeDtypeStruct((m, d), _BF)],
        compiler_params=_params("arbitrary"),
        name="outproj",
    )(og, od, orw, w, x, g)


def _ffn_kernel(h_ref, wg_ref, wu_ref, wd_ref, x_ref, g_ref, *rest, nf, emit_x):
    if emit_x:
        x2_ref, hn_ref, acc_ref = rest
    else:
        hn_ref, acc_ref = rest
    f = pl.program_id(1)

    @pl.when(f == 0)
    def _():
        acc_ref[...] = jnp.zeros_like(acc_ref)

    h = h_ref[...]
    gate = jnp.dot(h, wg_ref[...], preferred_element_type=_F32)
    up = jnp.dot(h, wu_ref[...], preferred_element_type=_F32)
    act = (_silu(gate) * up).astype(_BF)
    acc_ref[...] += jnp.dot(act, wd_ref[...], preferred_element_type=_F32)

    @pl.when(f == nf - 1)
    def _():
        x2 = x_ref[...] + acc_ref[...]
        if emit_x:
            x2_ref[...] = x2
        hn_ref[...] = _rmsnorm_rows(x2, g_ref[...]).astype(hn_ref.dtype)


def _ffn_call(h, wg, wu, wd, x, g, tm, tf, last):
    m, d = x.shape
    nf = D_FF // tf
    row = pl.BlockSpec((tm, d), lambda i, f: (i, 0))
    if last:
        out_specs = [row]
        out_shape = [jax.ShapeDtypeStruct((m, d), _F32)]
    else:
        out_specs = [row, row]
        out_shape = [jax.ShapeDtypeStruct((m, d), _F32), jax.ShapeDtypeStruct((m, d), _BF)]
    return pl.pallas_call(
        functools.partial(_ffn_kernel, nf=nf, emit_x=not last),
        grid=(m // tm, nf),
        in_specs=[row, pl.BlockSpec((d, tf), lambda i, f: (0, f)), pl.BlockSpec((d, tf), lambda i, f: (0, f)),
                  pl.BlockSpec((tf, d), lambda i, f: (f, 0)), row, pl.BlockSpec((1, d), lambda i, f: (0, 0))],
        out_specs=out_specs,
        out_shape=out_shape,
        scratch_shapes=[pltpu.VMEM((tm, d), _F32)],
        compiler_params=_params("arbitrary", "arbitrary"),
        name="ffn",
    )(h, wg, wu, wd, x, g)


def _gla_consts(c):
    nlev = int(np.log2(c))
    t = np.arange(c)
    blocks = [t[:, None] >= t[None, :],
              t[None, :] > t[:, None]]
    mq, mk, masks = [], [], []
    for lv in range(nlev):
        m = c >> lv
        half = m // 2
        blk, pos = t // m, t % m
        mid = blk * m + half
        upper = pos >= half
        mq.append(upper[:, None] & (t[None, :] >= mid[:, None]) & (t[None, :] <= t[:, None]))
        mk.append((~upper)[:, None] & (t[None, :] > t[:, None]) & (t[None, :] <= mid[:, None] - 1))
        masks.append((blk[:, None] == blk[None, :]) & upper[:, None] & (~upper)[None, :])
    mc = np.concatenate(blocks + mq + mk, axis=0).astype(np.float32)
    return jnp.asarray(mc, _BF), jnp.asarray(np.stack(masks).astype(np.float32)), nlev


def _gla_body(zg_ref, zs_ref, aup_ref, abias_ref, gn_ref, s0_ref, mc_ref, mk_ref,
              o_ref, sout_ref, s_scr, *, c, nlev, nc):
    step = pl.program_id(1)

    @pl.when(step == 0)
    def _():
        s_scr[...] = s0_ref[0]

    xa = _dot(zs_ref[...], aup_ref[...]) + abias_ref[...]
    la = (jnp.minimum(xa, 0.0) - jnp.log(1.0 + jnp.exp(-jnp.abs(xa)))) * (1.0 / GLA_GATE_TEMP)
    yield
    ex = jnp.exp(_cdot(mc_ref[...], la))
    yield
    q = zg_ref[:, 0:GLA_QK] * (GLA_DK ** -0.5)
    k = zg_ref[:, GLA_QK:2 * GLA_QK]
    qd = q * ex[0:c]
    kd = k * ex[c:2 * c]
    half = _lane_halves((c, LANES))
    eye = lax.broadcasted_iota(jnp.int32, (c, c), 0) == lax.broadcasted_iota(jnp.int32, (c, c), 1)
    ones = jnp.ones((c, LANES), _BF)
    heads = range(GLA_HEADS)
    sls = [slice(LANES * (h // 2), LANES * (h // 2 + 1)) for h in heads]
    sel = [(lambda x: jnp.where(half, x, 0.0)) if h % 2 == 0 else (lambda x: jnp.where(half, 0.0, x)) for h in heads]
    v = [zg_ref[:, 2 * GLA_QK + GLA_DV * h:2 * GLA_QK + GLA_DV * (h + 1)] for h in heads]
    s_old = [s_scr[p] for p in range(GLA_HEADS // 2)]
    a = [jnp.where(eye, _dot_nt(sel[h](q[:, sls[h]]), k[:, sls[h]]), 0.0) for h in heads]
    yield
    for lv in range(nlev):
        exq = ex[(2 + lv) * c:(3 + lv) * c]
        kx = k * ex[(2 + nlev + lv) * c:(3 + nlev + lv) * c]
        qx = q * exq
        a = [a[h] + mk_ref[lv] * _dot_nt(sel[h](qx[:, sls[h]]), kx[:, sls[h]]) for h in heads]
        yield
    o = [_dot(sel[h](qd[:, sls[h]]), s_old[h // 2]) + _dot(a[h], v[h]) for h in heads]
    yield
    for p in range(GLA_HEADS // 2):
        sl = sls[2 * p]
        s_scr[p] = (jnp.exp(_cdot_tn(la[:, sl], ones)) * s_old[p]
                    + _dot_tn(sel[0](kd[:, sl]), v[2 * p]) + _dot_tn(sel[1](kd[:, sl]), v[2 * p + 1]))
    yield
    for h in heads:
        g_h = zg_ref[:, 2 * GLA_QK + GLA_V + GLA_DV * h:2 * GLA_QK + GLA_V + GLA_DV * (h + 1)]
        y = _rmsnorm_rows(o[h], gn_ref[...]) * _silu(g_h)
        o_ref[:, GLA_DV * h:GLA_DV * (h + 1)] = y.astype(o_ref.dtype)

    @pl.when(step == nc - 1)
    def _():
        sout_ref[0] = s_scr[...]


class _Part:
    def __init__(self, body, args, in_specs, out_specs, out_shape, scratch):
        self.body, self.args, self.in_specs = body, list(args), list(in_specs)
        self.out_specs, self.out_shape, self.scratch = list(out_specs), list(out_shape), list(scratch)


def _full(a):
    return pl.BlockSpec(a.shape, lambda i, j: (0,) * a.ndim)


def _per_batch(a):
    return pl.BlockSpec((1,) + a.shape[1:], lambda i, j: (i,) + (0,) * (a.ndim - 1))


def _tok(c, nc, w):
    return pl.BlockSpec((c, w), lambda i, j: (i * nc + j, 0))


def _gla_part(zg, zs, aup, abias, gn, s0, b, t, c, nc):
    mc, masks, nlev = _gla_consts(c)
    return _Part(
        functools.partial(_gla_body, c=c, nlev=nlev, nc=nc),
        [zg, zs, aup, abias, gn, s0, mc, masks],
        [_tok(c, nc, ZG_W), _tok(c, nc, ZS_W), _full(aup), _full(abias), _full(gn), _per_batch(s0), _full(mc),
         _full(masks)],
        [_tok(c, nc, GLA_V), _per_batch(s0)],
        [jax.ShapeDtypeStruct((b * t, GLA_V), _BF), jax.ShapeDtypeStruct(s0.shape, _F32)],
        [pltpu.VMEM(s0.shape[1:], _F32)])


def _mixers_kernel(*refs, bodies, counts):
    n_in, n_out = sum(x[0] for x in counts), sum(x[1] for x in counts)
    i0, o0, s0 = 0, n_in, n_in + n_out
    gens = []
    for body, (ni, no, ns) in zip(bodies, counts):
        gens.append(body(*refs[i0:i0 + ni], *refs[o0:o0 + no], *refs[s0:s0 + ns]))
        i0, o0, s0 = i0 + ni, o0 + no, s0 + ns
    while gens:
        gens = [g for g in gens if next(g, StopIteration) is not StopIteration]


def _mixers_call(parts, b, nc):
    outs = pl.pallas_call(
        functools.partial(_mixers_kernel, bodies=[p.body for p in parts],
                          counts=[(len(p.args), len(p.out_shape), len(p.scratch)) for p in parts]),
        grid=(b, nc),
        in_specs=[s for p in parts for s in p.in_specs],
        out_specs=[s for p in parts for s in p.out_specs],
        out_shape=[s for p in parts for s in p.out_shape],
        scratch_shapes=[s for p in parts for s in p.scratch],
        compiler_params=_params("arbitrary", "arbitrary"),
        name="mixers",
    )(*[a for p in parts for a in p.args])
    res, k = [], 0
    for p in parts:
        res.append(outs[k:k + len(p.out_shape)])
        k += len(p.out_shape)
    return res


def _gdn_body(zd_ref, zs_ref, cw_ref, alog_ref, dtb_ref, gn_ref, cache_ref, s0_ref, tril_ref, triu_ref, mlast_ref,
              o_ref, cout_ref, sout_ref, cbuf, s_scr, *, c, nc):
    step = pl.program_id(1)
    tail0 = 8 - (CONV_W - 1)

    @pl.when(step == 0)
    def _():
        cbuf[tail0:8, :] = cache_ref[0]
        s_scr[...] = s0_ref[0]

    x = zd_ref[:, 0:GDN_CONV_CH]
    cbuf[8:8 + c, :] = x
    y = cbuf[tail0:tail0 + c, :] * cw_ref[0:1, :]
    for j in range(1, CONV_W - 1):
        y = y + cbuf[tail0 + j:tail0 + j + c, :] * cw_ref[j:j + 1, :]
    y = y + x * cw_ref[CONV_W - 1:CONV_W, :]
    tail = cbuf[c + tail0:c + 8, :]
    cbuf[tail0:8, :] = tail

    @pl.when(step == nc - 1)
    def _():
        cout_ref[0] = tail

    yield
    ys = _silu(y)
    yield
    zs = zs_ref[...]
    lg = -jnp.exp(alog_ref[...]) * _softplus(zs + dtb_ref[...])
    bt = _sigmoid(zs)
    bcum = _cdot(tril_ref[...], lg)
    bcum_t = _cdot_tn(lg, triu_ref[...])
    bdl = _cdot(mlast_ref[...], lg)
    btot_t = _cdot_tn(lg, jnp.ones((c, LANES), _BF))
    lane = lax.broadcasted_iota(jnp.int32, (c, LANES), 1)
    rows = lax.broadcasted_iota(jnp.int32, (c, LANES), 0)
    causal = (rows >= lane) & (lane < c)
    strict = (rows > lane) & (lane < c)
    heads = range(GDN_HEADS)
    pick = lambda tile, j: jnp.sum(jnp.where(lane == j, tile, 0.0), axis=1, keepdims=True)
    bcol = [pick(bcum, ZS_DT + h) for h in heads]
    beta = [pick(bt, ZS_BETA + h) for h in heads]
    dlcol = [pick(bdl, ZS_DT + h) for h in heads]
    dec = [jnp.exp(jnp.where(causal, bcol[h] - bcum_t[ZS_DT + h:ZS_DT + h + 1, :], _NEG)) for h in heads]
    eb = [jnp.exp(bcol[h]) for h in heads]
    yield
    q, k, v = [], [], []
    for h in heads:
        q_h = ys[:, GDN_DK * h:GDN_DK * (h + 1)]
        k_h = ys[:, GDN_QK + GDN_DK * h:GDN_QK + GDN_DK * (h + 1)]
        q.append(q_h * lax.rsqrt(jnp.sum(q_h * q_h, axis=-1, keepdims=True) + L2_EPS) * (GDN_DK ** -0.5))
        k.append(k_h * lax.rsqrt(jnp.sum(k_h * k_h, axis=-1, keepdims=True) + L2_EPS))
        v.append(ys[:, 2 * GDN_QK + GDN_DV * h:2 * GDN_QK + GDN_DV * (h + 1)])
    qk = [jnp.concatenate([q[h], k[h]], axis=0) for h in heads]
    s_old = [s_scr[h] for h in heads]
    yield
    sc = [_dot_nt(qk[h], _rows_at(k[h], 0)) for h in heads]
    ps = [_dot(qk[h], s_old[h]) for h in heads]
    yield
    ws = _inverse_start([jnp.where(strict, beta[h] * dec[h] * sc[h][c:], 0.0) for h in heads], c)
    for _ in range(int(np.log2(c))):
        ws = _inverse_step(ws, c)
        yield
    delta = [_solve_with(ws[h], beta[h] * (v[h] - eb[h] * ps[h][c:])) for h in heads]
    yield
    o = [eb[h] * ps[h][:c] + _dot(sc[h][:c] * dec[h], _rows_at(delta[h], 0)) for h in heads]
    for h in heads:
        elast = jnp.exp(btot_t[ZS_DT + h:ZS_DT + h + 1, :])
        s_scr[h] = elast * s_old[h] + _dot_tn(k[h] * jnp.exp(dlcol[h]), delta[h])
    yield
    for h in heads:
        g_h = zd_ref[:, GDN_CONV_CH + GDN_DV * h:GDN_CONV_CH + GDN_DV * (h + 1)]
        y_h = _rmsnorm_rows(o[h], gn_ref[...]) * _silu(g_h)
        o_ref[:, GDN_DV * h:GDN_DV * (h + 1)] = y_h.astype(o_ref.dtype)

    @pl.when(step == nc - 1)
    def _():
        sout_ref[0] = s_scr[...]


def _tri_consts(c):
    t = np.arange(c)
    tril = (t[:, None] >= t[None, :]).astype(np.float32)
    mlast = (t[None, :] > t[:, None]).astype(np.float32)
    triu_wide = np.pad(tril.T, ((0, 0), (0, LANES - c)))
    return jnp.asarray(tril, _BF), jnp.asarray(triu_wide, _BF), jnp.asarray(mlast, _BF)


def _gdn_part(zd, zs, cw, alog, dtb, gn, cache, s0, b, t, c, nc):
    tril, triu, mlast = _tri_consts(c)
    return _Part(
        functools.partial(_gdn_body, c=c, nc=nc),
        [zd, zs, cw, alog, dtb, gn, cache, s0, tril, triu, mlast],
        [_tok(c, nc, ZD_W), _tok(c, nc, ZS_W), _full(cw), _full(alog), _full(dtb), _full(gn), _per_batch(cache),
         _per_batch(s0), _full(tril), _full(triu), _full(mlast)],
        [_tok(c, nc, GDN_V), _per_batch(cache), _per_batch(s0)],
        [jax.ShapeDtypeStruct((b * t, GDN_V), _BF), jax.ShapeDtypeStruct(cache.shape, _F32),
         jax.ShapeDtypeStruct(s0.shape, _F32)],
        [pltpu.VMEM((c + 8, GDN_CONV_CH), _F32), pltpu.VMEM(s0.shape[1:], _F32)])


def _rw_body(*refs, c, nc, first):
    if first:
        (zr_ref, mu_ref, w0_ref, wup_ref, a0_ref, aup_ref, gup_ref, kk_ref, ka_ref, rk_ref, lng_ref, lnb_ref,
         cache_ref, s0_ref, tril_ref, mlast_ref, o_ref, vf_out_ref, shout_ref, sout_ref, sbuf, s_scr) = refs
    else:
        (zr_ref, vf_ref, v0_ref, vdown_ref, vup_ref, mu_ref, w0_ref, wup_ref, a0_ref, aup_ref, gup_ref, kk_ref,
         ka_ref, rk_ref, lng_ref, lnb_ref, cache_ref, s0_ref, tril_ref, mlast_ref, o_ref, shout_ref, sout_ref,
         sbuf, s_scr) = refs
    step = pl.program_id(1)

    @pl.when(step == 0)
    def _():
        sbuf[7:8, :] = cache_ref[0]
        s_scr[...] = s0_ref[0]

    x = zr_ref[...]
    sbuf[8:8 + c, :] = x
    zprev = sbuf[7:7 + c, :]
    last = sbuf[c + 7:c + 8, :]
    sbuf[7:8, :] = last

    @pl.when(step == nc - 1)
    def _():
        shout_ref[0] = last

    yield
    zm = x + (zprev - x) * mu_ref[...]
    xr, xk, xv = zm[:, 0:RW_C], zm[:, RW_C:2 * RW_C], zm[:, 2 * RW_C:3 * RW_C]
    xwa = zm[:, 3 * RW_C:3 * RW_C + LANES]
    xg = zm[:, 3 * RW_C + LANES:]
    wlog = -_softplus(-(w0_ref[...] + _dot(jnp.tanh(xwa), wup_ref[...]))) - 0.5
    lw = -jnp.exp(wlog)
    a = _sigmoid(a0_ref[...] + _dot(xwa, aup_ref[...]))
    yield
    if first:
        vf_out_ref[...] = xv
    else:
        nu = _sigmoid(v0_ref[...] + _dot(_dot(xv, vdown_ref[...]), vup_ref[...]))
        xv = xv + (vf_ref[...] - xv) * nu
    kkp = xk * kk_ref[...]
    xk2 = xk * (1.0 + (a - 1.0) * ka_ref[...])
    gate = _dot(_sigmoid(xg), gup_ref[...])
    cum = _cdot(tril_ref[...], lw)
    yield
    e_c = jnp.exp(cum)
    e_cp = jnp.exp(cum - lw)
    e_nc = jnp.exp(-cum)
    e_dl = jnp.exp(_cdot(mlast_ref[...], lw))
    half = _lane_halves((c, LANES))
    half2 = _lane_halves((2 * c, LANES))
    rows = lax.broadcasted_iota(jnp.int32, (c, LANES), 0)
    cols = lax.broadcasted_iota(jnp.int32, (c, LANES), 1) & (HALF - 1)
    strict = (rows > cols) & (cols < c)
    causal = (rows >= cols) & (cols < c)
    srow = lax.broadcasted_iota(jnp.int32, (LANES, LANES), 0) < (LANES // 2)
    scol = lax.broadcasted_iota(jnp.int32, (LANES, LANES), 1) < (LANES // 2)
    same_head = srow == scol
    ones = jnp.ones((c, LANES), _BF)
    pairs = range(RW_HEADS // 2)
    sls = [slice(LANES * p, LANES * (p + 1)) for p in pairs]
    r, k, v = [xr[:, s] for s in sls], [xk2[:, s] for s in sls], [xv[:, s] for s in sls]
    kap = [kkp[:, s] for s in sls]
    kap = [x * lax.rsqrt(_half_sum(x * x, half) + L2_EPS) for x in kap]
    ahat = [-(kap[p] * a[:, sls[p]]) for p in pairs]
    x2 = [jnp.concatenate([kap[p] * e_cp[:, sls[p]], r[p] * e_c[:, sls[p]]], axis=0) for p in pairs]
    at = [ahat[p] * e_nc[:, sls[p]] for p in pairs]
    kt = [k[p] * e_nc[:, sls[p]] for p in pairs]
    s_bd = [s_scr[p] for p in pairs]
    yield
    ps = [_dot(x2[p], s_bd[p]) for p in pairs]
    sc = []
    for p in pairs:
        sides = _rows_pair(at[p], kt[p])
        for xm in (jnp.where(half2, x2[p], 0.0), jnp.where(half2, 0.0, x2[p])):
            sc.append(_dot_nt(xm, sides))
    yield
    ws = _inverse_start([jnp.where(half & strict, -x[:c], 0.0) for x in sc], c)
    akv = [_dot(jnp.where(half, 0.0, jnp.where(strict, sc[i][:c], 0.0)), _rows_at(v[i // 2], HALF))
           for i in range(RW_HEADS)]
    rhs = [ps[p][:c] + jnp.where(half, akv[2 * p], akv[2 * p + 1]) for p in pairs]
    for _ in range(int(np.log2(c))):
        ws = _inverse_step(ws, c)
        yield
    u = [jnp.where(half, _solve_with(ws[2 * p], rhs[p]), _solve_with(ws[2 * p + 1], rhs[p])) for p in pairs]
    yield
    uv = [_rows_pair(u[p], v[p]) for p in pairs]
    oh = [_dot(jnp.where(causal, sc[i][c:], 0.0), uv[i // 2]) for i in range(RW_HEADS)]
    o = [ps[p][c:] + jnp.where(half, oh[2 * p], oh[2 * p + 1]) for p in pairs]
    yield
    for p in pairs:
        dl = e_dl[:, sls[p]]
        upd = _dot_tn(jnp.concatenate([ahat[p] * dl, k[p] * dl], axis=0), jnp.concatenate([u[p], v[p]], axis=0))
        s_scr[p] = jnp.exp(_cdot_tn(lw[:, sls[p]], ones)) * s_bd[p] + jnp.where(same_head, upd, 0.0)
    yield
    for p in pairs:
        mean = _half_sum(o[p], half) * (1.0 / RW_N)
        dev = o[p] - mean
        var = _half_sum(dev * dev, half) * (1.0 / RW_N)
        on = dev * lax.rsqrt(var + RW_GN_EPS) * lng_ref[:, sls[p]] + lnb_ref[:, sls[p]]
        bonus = _half_sum(r[p] * k[p] * rk_ref[:, sls[p]], half) * v[p]
        o_ref[:, sls[p]] = ((on + bonus) * gate[:, sls[p]]).astype(o_ref.dtype)

    @pl.when(step == nc - 1)
    def _():
        sout_ref[0] = s_scr[...]


def _rw_part(zr, vf, vmix, prm, cache, s0, b, t, c, nc):
    tril, _, mlast = _tri_consts(c)
    first = vf is None
    m = b * t
    args = [zr] + ([] if first else [vf] + list(vmix)) + list(prm) + [cache, s0, tril, mlast]
    in_specs = ([_tok(c, nc, RW_PROJ)] + ([] if first else [_tok(c, nc, RW_C)] + [_full(a) for a in vmix])
                + [_full(a) for a in prm] + [_per_batch(cache), _per_batch(s0), _full(tril), _full(mlast)])
    out_specs = [_tok(c, nc, RW_C)] + ([_tok(c, nc, RW_C)] if first else []) + [_per_batch(cache), _per_batch(s0)]
    out_shape = ([jax.ShapeDtypeStruct((m, RW_C), _BF)] + ([jax.ShapeDtypeStruct((m, RW_C), _F32)] if first else [])
                 + [jax.ShapeDtypeStruct(cache.shape, _F32), jax.ShapeDtypeStruct(s0.shape, _F32)])
    return _Part(functools.partial(_rw_body, c=c, nc=nc, first=first), args, in_specs, out_specs, out_shape,
                 [pltpu.VMEM((c + 8, RW_PROJ), _F32), pltpu.VMEM(s0.shape[1:], _F32)])


def _prep_weights(p):
    w_in = p["w_in"]
    o1 = GLA_PROJ
    o2 = GLA_PROJ + GDN_PROJ
    lora0 = 2 * GLA_QK + GLA_V
    w_gla = jnp.concatenate([w_in[:, :, 0:lora0], w_in[:, :, lora0 + GLA_LORA:o1]], axis=2)
    w_gdn = jnp.concatenate([w_in[:, :, o1:o1 + GDN_CONV_CH], w_in[:, :, o1 + GDN_CONV_CH + 2 * GDN_HEADS:o2]], axis=2)
    w_rw = w_in[:, :, o2:]
    narrow = jnp.concatenate([w_in[:, :, lora0:lora0 + GLA_LORA],
                              w_in[:, :, o1 + GDN_CONV_CH:o1 + GDN_CONV_CH + 2 * GDN_HEADS]], axis=2)
    w_small = jnp.pad(narrow, ((0, 0), (0, 0), (0, ZS_W - narrow.shape[2])))
    bf = lambda a: a.astype(_BF)
    pad_rows = lambda a, lo, hi: jnp.pad(a, ((0, 0), (lo, hi), (0, 0)))
    pad_lanes = lambda a, lo: jnp.pad(a, ((0, 0), (lo, ZS_W - lo - a.shape[1])))[:, None, :]
    row = lambda a: a.reshape(a.shape[0], 1, -1)
    return dict(
        w_gla=bf(w_gla), w_gdn=bf(w_gdn), w_rw=bf(w_rw), w_small=bf(w_small),
        w_out=bf(p["w_out"]), w_g=bf(p["w_ffn_gate"]), w_u=bf(p["w_ffn_up"]), w_d=bf(p["w_ffn_down"]),
        gla_aup=bf(pad_rows(p["gla_a_up"], 0, ZS_W - GLA_LORA)), gla_abias=row(p["gla_a_bias"]),
        gla_gn=row(p["gla_norm_g"]),
        gdn_cw=p["gdn_conv_w"], gdn_alog=pad_lanes(p["gdn_A_log"], ZS_DT), gdn_dtb=pad_lanes(p["gdn_dt_bias"], ZS_DT),
        gdn_gn=row(p["gdn_norm_g"]),
        rw_mu=row(p["rw_mu"]), rw_w0=row(p["rw_w0"]), rw_wup=bf(pad_rows(p["rw_w_up"], 0, LANES - RW_DECAY_LORA)),
        rw_a0=row(p["rw_a0"]), rw_aup=bf(pad_rows(p["rw_a_up"], RW_DECAY_LORA, 0)), rw_gup=bf(p["rw_g_up"]),
        rw_kk=row(p["rw_k_k"]), rw_ka=row(p["rw_k_a"]), rw_rk=p["rw_r_k"].reshape(DEPTH, 1, RW_C),
        rw_lng=row(p["rw_ln_g"]), rw_lnb=row(p["rw_ln_b"]),
        rw_v0=row(p["rw_v0"]), rw_vdown=bf(p["rw_v_down"]), rw_vup=bf(p["rw_v_up"]),
        norm1=row(p["norm1_g"]), norm2=row(p["norm2_g"]), final=p["final_norm_g"].reshape(1, -1),
    )


def _pair_block_diag(s):
    b, h, n, _ = s.shape
    s = s.reshape(b, h // 2, 2, n, n)
    z = jnp.zeros_like(s[:, :, 0])
    top = jnp.concatenate([s[:, :, 0], z], axis=-1)
    bot = jnp.concatenate([z, s[:, :, 1]], axis=-1)
    return jnp.concatenate([top, bot], axis=-2)


def _pair_blocks(s):
    b, hp, n2, _ = s.shape
    n = n2 // 2
    return jnp.stack([s[:, :, :n, :n], s[:, :, n:, n:]], axis=2).reshape(b, 2 * hp, n, n)


def _trunk(x, s_gla, s_gdn, c_gdn, s_rw, c_rw, w):
    b, t, d = x.shape
    m = b * t
    tm = min(512, m)
    c = min(CHUNK, t)
    nc = t // c
    xf = x.reshape(m, d)
    h = _norm_call(xf, w["norm1"][0], tm)
    outs = ([], [], [], [], [])
    vf = None
    for l in range(DEPTH):
        zg = _mm_call(h, w["w_gla"][l], tm, ZG_W // 2, "proj_gla")
        zd = _mm_call(h, w["w_gdn"][l], tm, ZD_W // 2, "proj_gdn")
        zr = _mm_call(h, w["w_rw"][l], tm, RW_PROJ // 2, "proj_rw")
        zs = _mm_call(h, w["w_small"][l], tm, ZS_W, "proj_small")
        prm = [w[n][l] for n in ("rw_mu", "rw_w0", "rw_wup", "rw_a0", "rw_aup", "rw_gup", "rw_kk", "rw_ka",
                                 "rw_rk", "rw_lng", "rw_lnb")]
        vmix = None if l == 0 else [w[n][l - 1] for n in ("rw_v0", "rw_vdown", "rw_vup")]
        parts = [
            _gla_part(zg, zs, w["gla_aup"][l], w["gla_abias"][l], w["gla_gn"][l],
                      s_gla[l].reshape(b, GLA_HEADS // 2, 2 * GLA_DK, GLA_DV), b, t, c, nc),
            _gdn_part(zd, zs, w["gdn_cw"][l], w["gdn_alog"][l], w["gdn_dtb"][l], w["gdn_gn"][l],
                      c_gdn[l], s_gdn[l], b, t, c, nc),
            _rw_part(zr, vf, vmix, prm, c_rw[l], _pair_block_diag(s_rw[l]), b, t, c, nc),
        ]
        (og, sg), (od, cg, sd), res = _mixers_call(parts, b, nc)
        if l == 0:
            orw, vf, cr, sr = res
        else:
            orw, cr, sr = res
        xf, h2 = _outproj_call(og, od, orw, w["w_out"][l], xf, w["norm2"][l], min(256, m))
        last = l == DEPTH - 1
        g_next = w["final"] if last else w["norm1"][l + 1]
        res = _ffn_call(h2, w["w_g"][l], w["w_u"][l], w["w_d"][l], xf, g_next, tm, 512, last)
        if last:
            (y,) = res
        else:
            xf, h = res
        for lst, arr in zip(outs, (sg.reshape(b, GLA_HEADS, GLA_DK, GLA_DV), sd, cg, _pair_blocks(sr), cr)):
            lst.append(arr)
    return y.reshape(b, t, d), [jnp.stack(lst) for lst in outs]


def kernel(x_prompt, x_sample, state_gla, state_gdn, cache_gdn_conv, state_rwkv, cache_rwkv_shift, norm1_g, w_in, gla_a_up, gla_a_bias, gla_norm_g, gdn_conv_w, gdn_A_log, gdn_dt_bias, gdn_norm_g, rw_mu, rw_w0, rw_w_up, rw_a0, rw_a_up, rw_v0, rw_v_down, rw_v_up, rw_g_up, rw_k_k, rw_k_a, rw_r_k, rw_ln_g, rw_ln_b, w_out, norm2_g, w_ffn_gate, w_ffn_up, w_ffn_down, final_norm_g):
    p = dict(norm1_g=norm1_g, w_in=w_in, gla_a_up=gla_a_up, gla_a_bias=gla_a_bias, gla_norm_g=gla_norm_g,
             gdn_conv_w=gdn_conv_w, gdn_A_log=gdn_A_log, gdn_dt_bias=gdn_dt_bias, gdn_norm_g=gdn_norm_g,
             rw_mu=rw_mu, rw_w0=rw_w0, rw_w_up=rw_w_up, rw_a0=rw_a0, rw_a_up=rw_a_up, rw_v0=rw_v0,
             rw_v_down=rw_v_down, rw_v_up=rw_v_up, rw_g_up=rw_g_up, rw_k_k=rw_k_k, rw_k_a=rw_k_a,
             rw_r_k=rw_r_k, rw_ln_g=rw_ln_g, rw_ln_b=rw_ln_b, w_out=w_out, norm2_g=norm2_g,
             w_ffn_gate=w_ffn_gate, w_ffn_up=w_ffn_up, w_ffn_down=w_ffn_down, final_norm_g=final_norm_g)
    w = _prep_weights(p)
    bp = x_prompt.shape[0]
    zeros = lambda a: jnp.zeros((DEPTH, bp) + a.shape[2:], a.dtype)
    y_p, st_p = _trunk(x_prompt, zeros(state_gla), zeros(state_gdn), zeros(cache_gdn_conv), zeros(state_rwkv),
                       zeros(cache_rwkv_shift), w)
    y_s, st_s = _trunk(x_sample, state_gla, state_gdn, cache_gdn_conv, state_rwkv, cache_rwkv_shift, w)
    return (y_p, y_s, *st_p, *st_s)
```

```python
import functools

import numpy as np
import jax
import jax.numpy as jnp
from jax import lax
from jax.experimental import pallas as pl
from jax.experimental.pallas import tpu as pltpu

D_MODEL = 2048
DEPTH = 4
CHUNK = 64
NORM_EPS = 1e-6
L2_EPS = 1e-6
GLA_HEADS, GLA_DK, GLA_DV, GLA_LORA, GLA_GATE_TEMP = 4, 64, 128, 16, 16.0
GDN_HEADS, GDN_DK, GDN_DV, CONV_W = 6, 128, 128, 4
RW_HEADS, RW_N = 12, 64
RW_DECAY_LORA, RW_AAA_LORA, RW_MV_LORA, RW_GATE_LORA = 64, 64, 32, 128
RW_GN_EPS = 64e-5

GLA_QK = GLA_HEADS * GLA_DK
GLA_V = GLA_HEADS * GLA_DV
GDN_QK = GDN_HEADS * GDN_DK
GDN_V = GDN_HEADS * GDN_DV
GDN_CONV_CH = 2 * GDN_QK + GDN_V
RW_C = RW_HEADS * RW_N
GLA_PROJ = 2 * GLA_QK + GLA_V + GLA_LORA + GLA_V
GDN_PROJ = GDN_CONV_CH + 2 * GDN_HEADS + GDN_V
RW_PROJ = 3 * RW_C + RW_DECAY_LORA + RW_AAA_LORA + RW_GATE_LORA
D_FF = -(-8 * D_MODEL // (3 * 256)) * 256

LANES = 128
ZG_W = 2 * GLA_QK + 2 * GLA_V
ZD_W = GDN_CONV_CH + GDN_V
ZS_W = LANES
ZS_BETA = GLA_LORA
ZS_DT = GLA_LORA + GDN_HEADS
VMEM_LIMIT = 52 * 1024 * 1024

_BF = jnp.bfloat16
_F32 = jnp.float32
_NEG = -1e30


def _dot(a, b):
    return jnp.dot(a.astype(_BF), b.astype(_BF), preferred_element_type=_F32)


def _dot_nt(a, b):
    return lax.dot_general(a.astype(_BF), b.astype(_BF), (((1,), (1,)), ((), ())),
                           preferred_element_type=_F32)


def _dot_tn(a, b):
    return lax.dot_general(a.astype(_BF), b.astype(_BF), (((0,), (0,)), ((), ())),
                           preferred_element_type=_F32)


def _split2(x):
    hi = x.astype(_BF)
    lo = (x - hi.astype(_F32)).astype(_BF)
    return hi, lo


def _cdot(c, x):
    return sum(jnp.dot(c, p, preferred_element_type=_F32) for p in _split2(x))


def _cdot_tn(x, c):
    return sum(lax.dot_general(p, c, (((0,), (0,)), ((), ())), preferred_element_type=_F32)
               for p in _split2(x))


def _sigmoid(x):
    return 1.0 / (1.0 + jnp.exp(-x))


def _silu(x):
    return x * _sigmoid(x)


def _softplus(x):
    return jnp.maximum(x, 0.0) + jnp.log(1.0 + jnp.exp(-jnp.abs(x)))


HALF = LANES // 2


def _rows_at(x, first, total=LANES):
    n, w = x.shape
    parts = [jnp.zeros((first, w), x.dtype)] if first else []
    parts.append(x)
    if total - first - n:
        parts.append(jnp.zeros((total - first - n, w), x.dtype))
    return jnp.concatenate(parts, axis=0) if len(parts) > 1 else x


def _rows_pair(a, b):
    return jnp.concatenate([_rows_at(a, 0, HALF), _rows_at(b, 0, HALF)], axis=0)


def _inverse_start(lows, n):
    rows = lax.broadcasted_iota(jnp.int32, (n, LANES), 0)
    lane = lax.broadcasted_iota(jnp.int32, (n, LANES), 1)
    return [jnp.where(lane == rows + HALF, 1.0, 0.0) - low for low in lows]


def _inverse_step(ws, n):
    keep = lax.broadcasted_iota(jnp.int32, (n, LANES), 1) >= HALF
    wbs = [w.astype(_BF) for w in ws]
    ys = [jnp.dot(wb, _rows_at(wb, 0), preferred_element_type=_F32) for wb in wbs]
    return [y + jnp.where(keep, w, 0.0) for y, w in zip(ys, ws)]


def _solve_with(w, rhs):
    wb = w.astype(_BF)
    return sum(jnp.dot(wb, _rows_at(part, HALF), preferred_element_type=_F32) for part in _split2(rhs))


def _lane_halves(shape):
    lane = lax.broadcasted_iota(jnp.int32, shape, len(shape) - 1)
    return lane < (LANES // 2)


def _half_sum(x, half):
    s0 = jnp.sum(jnp.where(half, x, 0.0), axis=-1, keepdims=True)
    s1 = jnp.sum(jnp.where(half, 0.0, x), axis=-1, keepdims=True)
    return jnp.where(half, s0, s1)


def _rmsnorm_rows(x, g):
    return x * lax.rsqrt(jnp.mean(x * x, axis=-1, keepdims=True) + NORM_EPS) * g


def _params(*sem):
    return pltpu.CompilerParams(dimension_semantics=sem, vmem_limit_bytes=VMEM_LIMIT)


def _norm_kernel(x_ref, g_ref, o_ref):
    o_ref[...] = _rmsnorm_rows(x_ref[...], g_ref[...]).astype(o_ref.dtype)


def _norm_call(x, g, tm):
    m, d = x.shape
    return pl.pallas_call(
        _norm_kernel,
        grid=(m // tm,),
        in_specs=[pl.BlockSpec((tm, d), lambda i: (i, 0)), pl.BlockSpec((1, d), lambda i: (0, 0))],
        out_specs=pl.BlockSpec((tm, d), lambda i: (i, 0)),
        out_shape=jax.ShapeDtypeStruct((m, d), _BF),
        compiler_params=_params("arbitrary"),
        name="rmsnorm",
    )(x, g)


def _mm_kernel(a_ref, w_ref, o_ref):
    o_ref[...] = jnp.dot(a_ref[...], w_ref[...], preferred_element_type=_F32)


def _mm_call(a, w, tm, tn, name):
    m, k = a.shape
    n = w.shape[1]
    return pl.pallas_call(
        _mm_kernel,
        grid=(n // tn, m // tm),
        in_specs=[pl.BlockSpec((tm, k), lambda j, i: (i, 0)), pl.BlockSpec((k, tn), lambda j, i: (0, j))],
        out_specs=pl.BlockSpec((tm, tn), lambda j, i: (i, j)),
        out_shape=jax.ShapeDtypeStruct((m, n), _F32),
        compiler_params=_params("arbitrary", "arbitrary"),
        name=name,
    )(a, w)


def _outproj_kernel(og_ref, od_ref, or_ref, w_ref, x_ref, g_ref, x1_ref, h_ref):
    d = functools.partial(jnp.dot, preferred_element_type=_F32)
    acc = (d(og_ref[...], w_ref[0:GLA_V, :]) + d(od_ref[...], w_ref[GLA_V:GLA_V + GDN_V, :])
           + d(or_ref[...], w_ref[GLA_V + GDN_V:, :]))
    x1 = x_ref[...] + acc
    x1_ref[...] = x1
    h_ref[...] = _rmsnorm_rows(x1, g_ref[...]).astype(h_ref.dtype)


def _outproj_call(og, od, orw, w, x, g, tm):
    m, d = x.shape
    row = lambda width: pl.BlockSpec((tm, width), lambda i: (i, 0))
    return pl.pallas_call(
        _outproj_kernel,
        grid=(m // tm,),
        in_specs=[row(GLA_V), row(GDN_V), row(RW_C), pl.BlockSpec(w.shape, lambda i: (0, 0)), row(d),
                  pl.BlockSpec((1, d), lambda i: (0, 0))],
        out_specs=[row(d), row(d)],
        out_shape=[jax.ShapeDtypeStruct((m, d), _F32), jax.ShapeDtypeStruct((m, d), _BF)],
        compiler_params=_params("arbitrary"),
        name="outproj",
    )(og, od, orw, w, x, g)


def _ffn_kernel(h_ref, wg_ref, wu_ref, wd_ref, x_ref, g_ref, *rest, nf, emit_x):
    if emit_x:
        x2_ref, hn_ref, acc_ref = rest
    else:
        hn_ref, acc_ref = rest
    f = pl.program_id(1)

    @pl.when(f == 0)
    def _():
        acc_ref[...] = jnp.zeros_like(acc_ref)

    h = h_ref[...]
    gate = jnp.dot(h, wg_ref[...], preferred_element_type=_F32)
    up = jnp.dot(h, wu_ref[...], preferred_element_type=_F32)
    act = (_silu(gate) * up).astype(_BF)
    acc_ref[...] += jnp.dot(act, wd_ref[...], preferred_element_type=_F32)

    @pl.when(f == nf - 1)
    def _():
        x2 = x_ref[...] + acc_ref[...]
        if emit_x:
            x2_ref[...] = x2
        hn_ref[...] = _rmsnorm_rows(x2, g_ref[...]).astype(hn_ref.dtype)


def _ffn_call(h, wg, wu, wd, x, g, tm, tf, last):
    m, d = x.shape
    nf = D_FF // tf
    row = pl.BlockSpec((tm, d), lambda i, f: (i, 0))
    if last:
        out_specs = [row]
        out_shape = [jax.ShapeDtypeStruct((m, d), _F32)]
    else:
        out_specs = [row, row]
        out_shape = [jax.ShapeDtypeStruct((m, d), _F32), jax.ShapeDtypeStruct((m, d), _BF)]
    return pl.pallas_call(
        functools.partial(_ffn_kernel, nf=nf, emit_x=not last),
        grid=(m // tm, nf),
        in_specs=[row, pl.BlockSpec((d, tf), lambda i, f: (0, f)), pl.BlockSpec((d, tf), lambda i, f: (0, f)),
                  pl.BlockSpec((tf, d), lambda i, f: (f, 0)), row, pl.BlockSpec((1, d), lambda i, f: (0, 0))],
        out_specs=out_specs,
        out_shape=out_shape,
        scratch_shapes=[pltpu.VMEM((tm, d), _F32)],
        compiler_params=_params("arbitrary", "arbitrary"),
        name="ffn",
    )(h, wg, wu, wd, x, g)


def _gla_consts(c):
    nlev = int(np.log2(c))
    t = np.arange(c)
    blocks = [t[:, None] >= t[None, :],
              t[None, :] > t[:, None]]
    mq, mk, masks = [], [], []
    for lv in range(nlev):
        m = c >> lv
        half = m // 2
        blk, pos = t // m, t % m
        mid = blk * m + half
        upper = pos >= half
        mq.append(upper[:, None] & (t[None, :] >= mid[:, None]) & (t[None, :] <= t[:, None]))
        mk.append((~upper)[:, None] & (t[None, :] > t[:, None]) & (t[None, :] <= mid[:, None] - 1))
        masks.append((blk[:, None] == blk[None, :]) & upper[:, None] & (~upper)[None, :])
    mc = np.concatenate(blocks + mq + mk, axis=0).astype(np.float32)
    return jnp.asarray(mc, _BF), jnp.asarray(np.stack(masks).astype(np.float32)), nlev


def _gla_body(zg_ref, zs_ref, aup_ref, abias_ref, gn_ref, s0_ref, mc_ref, mk_ref,
              o_ref, sout_ref, s_scr, *, c, nlev, nc):
    step = pl.program_id(1)

    @pl.when(step == 0)
    def _():
        s_scr[...] = s0_ref[...]

    xa = _dot(zs_ref[...], aup_ref[...]) + abias_ref[...]
    la = (jnp.minimum(xa, 0.0) - jnp.log(1.0 + jnp.exp(-jnp.abs(xa)))) * (1.0 / GLA_GATE_TEMP)
    yield
    ex = jnp.exp(_cdot(mc_ref[...], la))
    yield
    q = zg_ref[:, 0:GLA_QK] * (GLA_DK ** -0.5)
    k = zg_ref[:, GLA_QK:2 * GLA_QK]
    qd = q * ex[0:c]
    kd = k * ex[c:2 * c]
    half = _lane_halves((c, LANES))
    eye = lax.broadcasted_iota(jnp.int32, (c, c), 0) == lax.broadcasted_iota(jnp.int32, (c, c), 1)
    ones = jnp.ones((c, LANES), _BF)
    heads = range(GLA_HEADS)
    sls = [slice(LANES * (h // 2), LANES * (h // 2 + 1)) for h in heads]
    sel = [(lambda x: jnp.where(half, x, 0.0)) if h % 2 == 0 else (lambda x: jnp.where(half, 0.0, x)) for h in heads]
    v = [zg_ref[:, 2 * GLA_QK + GLA_DV * h:2 * GLA_QK + GLA_DV * (h + 1)] for h in heads]
    s_old = [s_scr[p] for p in range(GLA_HEADS // 2)]
    a = [jnp.where(eye, _dot_nt(sel[h](q[:, sls[h]]), k[:, sls[h]]), 0.0) for h in heads]
    yield
    for lv in range(nlev):
        exq = ex[(2 + lv) * c:(3 + lv) * c]
        kx = k * ex[(2 + nlev + lv) * c:(3 + nlev + lv) * c]
        qx = q * exq
        a = [a[h] + mk_ref[lv] * _dot_nt(sel[h](qx[:, sls[h]]), kx[:, sls[h]]) for h in heads]
        yield
    o = [_dot(sel[h](qd[:, sls[h]]), s_old[h // 2]) + _dot(a[h], v[h]) for h in heads]
    yield
    for p in range(GLA_HEADS // 2):
        sl = sls[2 * p]
        s_scr[p] = (jnp.exp(_cdot_tn(la[:, sl], ones)) * s_old[p]
                    + _dot_tn(sel[0](kd[:, sl]), v[2 * p]) + _dot_tn(sel[1](kd[:, sl]), v[2 * p + 1]))
    yield
    for h in heads:
        g_h = zg_ref[:, 2 * GLA_QK + GLA_V + GLA_DV * h:2 * GLA_QK + GLA_V + GLA_DV * (h + 1)]
        y = _rmsnorm_rows(o[h], gn_ref[...]) * _silu(g_h)
        o_ref[:, GLA_DV * h:GLA_DV * (h + 1)] = y.astype(o_ref.dtype)

    @pl.when(step == nc - 1)
    def _():
        sout_ref[...] = s_scr[...]


ROWS = 1
ROW_LAG = 6


class _Part:
    def __init__(self, body, args, in_specs, out_specs, out_shape, scratch):
        self.body, self.args = body, list(args)
        self.in_specs, self.per_row = [s for s, _ in in_specs], [r for _, r in in_specs]
        self.out_specs, self.out_shape = [s for s, _ in out_specs], list(out_shape)
        self.scratch = [pltpu.VMEM((ROWS,) + shape, dtype) for shape, dtype in scratch]


def _full(a):
    return pl.BlockSpec(a.shape, lambda i, j: (0,) * a.ndim), False


def _per_batch(a):
    return pl.BlockSpec((ROWS,) + a.shape[1:], lambda i, j: (i,) + (0,) * (a.ndim - 1)), True


def _tok(c, w, col=0):
    return pl.BlockSpec((ROWS, c, w), lambda i, j: (i, j, col)), True


ZS_COL = ZG_W // ZS_W


def _gla_part(zg, zs, aup, abias, gn, s0, b, t, c, nc):
    mc, masks, nlev = _gla_consts(c)
    return _Part(
        functools.partial(_gla_body, c=c, nlev=nlev, nc=nc),
        [zg.reshape(b, t, -1), zs.reshape(b, t, -1), aup, abias, gn, s0, mc, masks],
        [_tok(c, ZG_W), _tok(c, ZS_W, ZS_COL), _full(aup), _full(abias), _full(gn), _per_batch(s0), _full(mc),
         _full(masks)],
        [_tok(c, GLA_V), _per_batch(s0)],
        [jax.ShapeDtypeStruct((b, t, GLA_V), _BF), jax.ShapeDtypeStruct(s0.shape, _F32)],
        [(s0.shape[1:], _F32)])


def _mixers_kernel(*refs, bodies, counts, per_row):
    n_in, n_out = sum(x[0] for x in counts), sum(x[1] for x in counts)
    rows = []
    for r in range(ROWS):
        i0, o0, s0 = 0, n_in, n_in + n_out
        gens = []
        for body, (ni, no, ns), flags in zip(bodies, counts, per_row):
            ins = [ref.at[r] if f else ref for ref, f in zip(refs[i0:i0 + ni], flags)]
            rest = [ref.at[r] for ref in refs[o0:o0 + no] + refs[s0:s0 + ns]]
            gens.append(body(*ins, *rest))
            i0, o0, s0 = i0 + ni, o0 + no, s0 + ns
        rows.append(gens)
    active, rounds = [], 0
    while active or rows:
        if rows and rounds % ROW_LAG == 0:
            active += rows.pop(0)
        active = [g for g in active if next(g, StopIteration) is not StopIteration]
        rounds += 1


def _mixers_call(parts, b, nc):
    outs = pl.pallas_call(
        functools.partial(_mixers_kernel, bodies=[p.body for p in parts],
                          counts=[(len(p.args), len(p.out_shape), len(p.scratch)) for p in parts],
                          per_row=[p.per_row for p in parts]),
        grid=(b // ROWS, nc),
        in_specs=[s for p in parts for s in p.in_specs],
        out_specs=[s for p in parts for s in p.out_specs],
        out_shape=[s for p in parts for s in p.out_shape],
        scratch_shapes=[s for p in parts for s in p.scratch],
        compiler_params=_params("arbitrary", "arbitrary"),
        name="mixers",
    )(*[a for p in parts for a in p.args])
    res, k = [], 0
    for p in parts:
        res.append(outs[k:k + len(p.out_shape)])
        k += len(p.out_shape)
    return res


def _gdn_body(zd_ref, zs_ref, cw_ref, alog_ref, dtb_ref, gn_ref, cache_ref, s0_ref, tril_ref, triu_ref, mlast_ref,
              o_ref, cout_ref, sout_ref, cbuf, s_scr, *, c, nc):
    step = pl.program_id(1)
    tail0 = 8 - (CONV_W - 1)

    @pl.when(step == 0)
    def _():
        cbuf[tail0:8, :] = cache_ref[...]
        s_scr[...] = s0_ref[...]

    x = zd_ref[:, 0:GDN_CONV_CH]
    cbuf[8:8 + c, :] = x
    y = cbuf[tail0:tail0 + c, :] * cw_ref[0:1, :]
    for j in range(1, CONV_W - 1):
        y = y + cbuf[tail0 + j:tail0 + j + c, :] * cw_ref[j:j + 1, :]
    y = y + x * cw_ref[CONV_W - 1:CONV_W, :]
    tail = cbuf[c + tail0:c + 8, :]
    cbuf[tail0:8, :] = tail

    @pl.when(step == nc - 1)
    def _():
        cout_ref[...] = tail

    yield
    ys = _silu(y)
    yield
    zs = zs_ref[...]
    lg = -jnp.exp(alog_ref[...]) * _softplus(zs + dtb_ref[...])
    bt = _sigmoid(zs)
    bcum = _cdot(tril_ref[...], lg)
    bcum_t = _cdot_tn(lg, triu_ref[...])
    bdl = _cdot(mlast_ref[...], lg)
    btot_t = _cdot_tn(lg, jnp.ones((c, LANES), _BF))
    lane = lax.broadcasted_iota(jnp.int32, (c, LANES), 1)
    rows = lax.broadcasted_iota(jnp.int32, (c, LANES), 0)
    causal = (rows >= lane) & (lane < c)
    strict = (rows > lane) & (lane < c)
    heads = range(GDN_HEADS)
    pick = lambda tile, j: jnp.sum(jnp.where(lane == j, tile, 0.0), axis=1, keepdims=True)
    bcol = [pick(bcum, ZS_DT + h) for h in heads]
    beta = [pick(bt, ZS_BETA + h) for h in heads]
    dlcol = [pick(bdl, ZS_DT + h) for h in heads]
    dec = [jnp.exp(jnp.where(causal, bcol[h] - bcum_t[ZS_DT + h:ZS_DT + h + 1, :], _NEG)) for h in heads]
    eb = [jnp.exp(bcol[h]) for h in heads]
    yield
    q, k, v = [], [], []
    for h in heads:
        q_h = ys[:, GDN_DK * h:GDN_DK * (h + 1)]
        k_h = ys[:, GDN_QK + GDN_DK * h:GDN_QK + GDN_DK * (h + 1)]
        q.append(q_h * lax.rsqrt(jnp.sum(q_h * q_h, axis=-1, keepdims=True) + L2_EPS) * (GDN_DK ** -0.5))
        k.append(k_h * lax.rsqrt(jnp.sum(k_h * k_h, axis=-1, keepdims=True) + L2_EPS))
        v.append(ys[:, 2 * GDN_QK + GDN_DV * h:2 * GDN_QK + GDN_DV * (h + 1)])
    qk = [jnp.concatenate([q[h], k[h]], axis=0) for h in heads]
    s_old = [s_scr[h] for h in heads]
    yield
    sc = [_dot_nt(qk[h], _rows_at(k[h], 0)) for h in heads]
    ps = [_dot(qk[h], s_old[h]) for h in heads]
    yield
    ws = _inverse_start([jnp.where(strict, beta[h] * dec[h] * sc[h][c:], 0.0) for h in heads], c)
    for _ in range(int(np.log2(c))):
        ws = _inverse_step(ws, c)
        yield
    delta = [_solve_with(ws[h], beta[h] * (v[h] - eb[h] * ps[h][c:])) for h in heads]
    yield
    o = [eb[h] * ps[h][:c] + _dot(sc[h][:c] * dec[h], _rows_at(delta[h], 0)) for h in heads]
    for h in heads:
        elast = jnp.exp(btot_t[ZS_DT + h:ZS_DT + h + 1, :])
        s_scr[h] = elast * s_old[h] + _dot_tn(k[h] * jnp.exp(dlcol[h]), delta[h])
    yield
    for h in heads:
        g_h = zd_ref[:, GDN_CONV_CH + GDN_DV * h:GDN_CONV_CH + GDN_DV * (h + 1)]
        y_h = _rmsnorm_rows(o[h], gn_ref[...]) * _silu(g_h)
        o_ref[:, GDN_DV * h:GDN_DV * (h + 1)] = y_h.astype(o_ref.dtype)

    @pl.when(step == nc - 1)
    def _():
        sout_ref[...] = s_scr[...]


def _tri_consts(c):
    t = np.arange(c)
    tril = (t[:, None] >= t[None, :]).astype(np.float32)
    mlast = (t[None, :] > t[:, None]).astype(np.float32)
    triu_wide = np.pad(tril.T, ((0, 0), (0, LANES - c)))
    return jnp.asarray(tril, _BF), jnp.asarray(triu_wide, _BF), jnp.asarray(mlast, _BF)


def _gdn_part(zd, zs, cw, alog, dtb, gn, cache, s0, b, t, c, nc):
    tril, triu, mlast = _tri_consts(c)
    return _Part(
        functools.partial(_gdn_body, c=c, nc=nc),
        [zd.reshape(b, t, -1), zs.reshape(b, t, -1), cw, alog, dtb, gn, cache, s0, tril, triu, mlast],
        [_tok(c, ZD_W), _tok(c, ZS_W, ZS_COL), _full(cw), _full(alog), _full(dtb), _full(gn), _per_batch(cache),
         _per_batch(s0), _full(tril), _full(triu), _full(mlast)],
        [_tok(c, GDN_V), _per_batch(cache), _per_batch(s0)],
        [jax.ShapeDtypeStruct((b, t, GDN_V), _BF), jax.ShapeDtypeStruct(cache.shape, _F32),
         jax.ShapeDtypeStruct(s0.shape, _F32)],
        [((c + 8, GDN_CONV_CH), _F32), (s0.shape[1:], _F32)])


def _rw_body(*refs, c, nc, first):
    if first:
        (zr_ref, mu_ref, w0_ref, wup_ref, a0_ref, aup_ref, gup_ref, kk_ref, ka_ref, rk_ref, lng_ref, lnb_ref,
         cache_ref, s0_ref, tril_ref, mlast_ref, o_ref, vf_out_ref, shout_ref, sout_ref, sbuf, s_scr) = refs
    else:
        (zr_ref, vf_ref, v0_ref, vdown_ref, vup_ref, mu_ref, w0_ref, wup_ref, a0_ref, aup_ref, gup_ref, kk_ref,
         ka_ref, rk_ref, lng_ref, lnb_ref, cache_ref, s0_ref, tril_ref, mlast_ref, o_ref, shout_ref, sout_ref,
         sbuf, s_scr) = refs
    step = pl.program_id(1)

    @pl.when(step == 0)
    def _():
        sbuf[7:8, :] = cache_ref[...]
        s_scr[...] = s0_ref[...]

    x = zr_ref[...]
    sbuf[8:8 + c, :] = x
    zprev = sbuf[7:7 + c, :]
    last = sbuf[c + 7:c + 8, :]
    sbuf[7:8, :] = last

    @pl.when(step == nc - 1)
    def _():
        shout_ref[...] = last

    yield
    zm = x + (zprev - x) * mu_ref[...]
    xr, xk, xv = zm[:, 0:RW_C], zm[:, RW_C:2 * RW_C], zm[:, 2 * RW_C:3 * RW_C]
    xwa = zm[:, 3 * RW_C:3 * RW_C + LANES]
    xg = zm[:, 3 * RW_C + LANES:]
    wlog = -_softplus(-(w0_ref[...] + _dot(jnp.tanh(xwa), wup_ref[...]))) - 0.5
    lw = -jnp.exp(wlog)
    a = _sigmoid(a0_ref[...] + _dot(xwa, aup_ref[...]))
    yield
    if first:
        vf_out_ref[...] = xv
    else:
        nu = _sigmoid(v0_ref[...] + _dot(_dot(xv, vdown_ref[...]), vup_ref[...]))
        xv = xv + (vf_ref[...] - xv) * nu
    kkp = xk * kk_ref[...]
    xk2 = xk * (1.0 + (a - 1.0) * ka_ref[...])
    gate = _dot(_sigmoid(xg), gup_ref[...])
    cum = _cdot(tril_ref[...], lw)
    yield
    e_c = jnp.exp(cum)
    e_cp = jnp.exp(cum - lw)
    e_nc = jnp.exp(-cum)
    e_dl = jnp.exp(_cdot(mlast_ref[...], lw))
    half = _lane_halves((c, LANES))
    half2 = _lane_halves((2 * c, LANES))
    rows = lax.broadcasted_iota(jnp.int32, (c, LANES), 0)
    cols = lax.broadcasted_iota(jnp.int32, (c, LANES), 1) & (HALF - 1)
    strict = (rows > cols) & (cols < c)
    causal = (rows >= cols) & (cols < c)
    srow = lax.broadcasted_iota(jnp.int32, (LANES, LANES), 0) < (LANES // 2)
    scol = lax.broadcasted_iota(jnp.int32, (LANES, LANES), 1) < (LANES // 2)
    same_head = srow == scol
    ones = jnp.ones((c, LANES), _BF)
    pairs = range(RW_HEADS // 2)
    sls = [slice(LANES * p, LANES * (p + 1)) for p in pairs]
    r, k, v = [xr[:, s] for s in sls], [xk2[:, s] for s in sls], [xv[:, s] for s in sls]
    kap = [kkp[:, s] for s in sls]
    kap = [x * lax.rsqrt(_half_sum(x * x, half) + L2_EPS) for x in kap]
    ahat = [-(kap[p] * a[:, sls[p]]) for p in pairs]
    x2 = [jnp.concatenate([kap[p] * e_cp[:, sls[p]], r[p] * e_c[:, sls[p]]], axis=0) for p in pairs]
    at = [ahat[p] * e_nc[:, sls[p]] for p in pairs]
    kt = [k[p] * e_nc[:, sls[p]] for p in pairs]
    s_bd = [s_scr[p] for p in pairs]
    yield
    ps = [_dot(x2[p], s_bd[p]) for p in pairs]
    sc = []
    for p in pairs:
        sides = _rows_pair(at[p], kt[p])
        x2b = x2[p].astype(_BF)
        for xm in (jnp.where(half2, x2b, 0), jnp.where(half2, 0, x2b)):
            sc.append(_dot_nt(xm, sides))
    yield
    ws = _inverse_start([jnp.where(half & strict, -x[:c], 0.0) for x in sc], c)
    akv = [_dot(jnp.where(half, 0.0, jnp.where(strict, sc[i][:c], 0.0)), _rows_at(v[i // 2], HALF))
           for i in range(RW_HEADS)]
    rhs = [ps[p][:c] + jnp.where(half, akv[2 * p], akv[2 * p + 1]) for p in pairs]
    for _ in range(int(np.log2(c))):
        ws = _inverse_step(ws, c)
        yield
    u = [jnp.where(half, _solve_with(ws[2 * p], rhs[p]), _solve_with(ws[2 * p + 1], rhs[p])) for p in pairs]
    yield
    uv = [_rows_pair(u[p], v[p]) for p in pairs]
    oh = [_dot(jnp.where(causal, sc[i][c:], 0.0), uv[i // 2]) for i in range(RW_HEADS)]
    o = [ps[p][c:] + jnp.where(half, oh[2 * p], oh[2 * p + 1]) for p in pairs]
    yield
    for p in pairs:
        dl = e_dl[:, sls[p]]
        upd = _dot_tn(jnp.concatenate([ahat[p] * dl, k[p] * dl], axis=0), jnp.concatenate([u[p], v[p]], axis=0))
        s_scr[p] = jnp.exp(_cdot_tn(lw[:, sls[p]], ones)) * s_bd[p] + jnp.where(same_head, upd, 0.0)
    yield
    for p in pairs:
        mean = _half_sum(o[p], half) * (1.0 / RW_N)
        dev = o[p] - mean
        var = _half_sum(dev * dev, half) * (1.0 / RW_N)
        on = dev * lax.rsqrt(var + RW_GN_EPS) * lng_ref[:, sls[p]] + lnb_ref[:, sls[p]]
        bonus = _half_sum(r[p] * k[p] * rk_ref[:, sls[p]], half) * v[p]
        o_ref[:, sls[p]] = ((on + bonus) * gate[:, sls[p]]).astype(o_ref.dtype)

    @pl.when(step == nc - 1)
    def _():
        sout_ref[...] = s_scr[...]


def _rw_part(zr, vf, vmix, prm, cache, s0, b, t, c, nc):
    tril, _, mlast = _tri_consts(c)
    first = vf is None
    args = ([zr.reshape(b, t, -1)] + ([] if first else [vf.reshape(b, t, -1)] + list(vmix)) + list(prm)
            + [cache, s0, tril, mlast])
    in_specs = ([_tok(c, RW_PROJ)] + ([] if first else [_tok(c, RW_C)] + [_full(a) for a in vmix])
                + [_full(a) for a in prm] + [_per_batch(cache), _per_batch(s0), _full(tril), _full(mlast)])
    out_specs = [_tok(c, RW_C)] + ([_tok(c, RW_C)] if first else []) + [_per_batch(cache), _per_batch(s0)]
    out_shape = ([jax.ShapeDtypeStruct((b, t, RW_C), _BF)] + ([jax.ShapeDtypeStruct((b, t, RW_C), _F32)] if first else [])
                 + [jax.ShapeDtypeStruct(cache.shape, _F32), jax.ShapeDtypeStruct(s0.shape, _F32)])
    return _Part(functools.partial(_rw_body, c=c, nc=nc, first=first), args, in_specs, out_specs, out_shape,
                 [((c + 8, RW_PROJ), _F32), (s0.shape[1:], _F32)])


def _prep_weights(p):
    bf = lambda a: a.astype(_BF)
    w_in = bf(p["w_in"])
    o1 = GLA_PROJ
    o2 = GLA_PROJ + GDN_PROJ
    lora0 = 2 * GLA_QK + GLA_V
    narrow = jnp.concatenate([w_in[:, :, lora0:lora0 + GLA_LORA],
                              w_in[:, :, o1 + GDN_CONV_CH:o1 + GDN_CONV_CH + 2 * GDN_HEADS]], axis=2)
    w_small = jnp.pad(narrow, ((0, 0), (0, 0), (0, ZS_W - narrow.shape[2])))
    w_gla = jnp.concatenate([w_in[:, :, 0:lora0], w_in[:, :, lora0 + GLA_LORA:o1], w_small], axis=2)
    w_gdn = jnp.concatenate([w_in[:, :, o1:o1 + GDN_CONV_CH], w_in[:, :, o1 + GDN_CONV_CH + 2 * GDN_HEADS:o2]], axis=2)
    w_rw = w_in[:, :, o2:]
    pad_rows = lambda a, lo, hi: jnp.pad(a, ((0, 0), (lo, hi), (0, 0)))
    pad_lanes = lambda a, lo: jnp.pad(a, ((0, 0), (lo, ZS_W - lo - a.shape[1])))[:, None, :]
    row = lambda a: a.reshape(a.shape[0], 1, -1)
    return dict(
        w_gla=w_gla, w_gdn=w_gdn, w_rw=w_rw,
        w_out=bf(p["w_out"]), w_g=bf(p["w_ffn_gate"]), w_u=bf(p["w_ffn_up"]), w_d=bf(p["w_ffn_down"]),
        gla_aup=bf(pad_rows(p["gla_a_up"], 0, ZS_W - GLA_LORA)), gla_abias=row(p["gla_a_bias"]),
        gla_gn=row(p["gla_norm_g"]),
        gdn_cw=p["gdn_conv_w"], gdn_alog=pad_lanes(p["gdn_A_log"], ZS_DT), gdn_dtb=pad_lanes(p["gdn_dt_bias"], ZS_DT),
        gdn_gn=row(p["gdn_norm_g"]),
        rw_mu=row(p["rw_mu"]), rw_w0=row(p["rw_w0"]), rw_wup=bf(pad_rows(p["rw_w_up"], 0, LANES - RW_DECAY_LORA)),
        rw_a0=row(p["rw_a0"]), rw_aup=bf(pad_rows(p["rw_a_up"], RW_DECAY_LORA, 0)), rw_gup=bf(p["rw_g_up"]),
        rw_kk=row(p["rw_k_k"]), rw_ka=row(p["rw_k_a"]), rw_rk=p["rw_r_k"].reshape(DEPTH, 1, RW_C),
        rw_lng=row(p["rw_ln_g"]), rw_lnb=row(p["rw_ln_b"]),
        rw_v0=row(p["rw_v0"]), rw_vdown=bf(p["rw_v_down"]), rw_vup=bf(p["rw_v_up"]),
        norm1=row(p["norm1_g"]), norm2=row(p["norm2_g"]), final=p["final_norm_g"].reshape(1, -1),
    )


def _pair_block_diag(s):
    b, h, n, _ = s.shape
    s = s.reshape(b, h // 2, 2, n, n)
    z = jnp.zeros_like(s[:, :, 0])
    top = jnp.concatenate([s[:, :, 0], z], axis=-1)
    bot = jnp.concatenate([z, s[:, :, 1]], axis=-1)
    return jnp.concatenate([top, bot], axis=-2)


def _pair_blocks(s):
    b, hp, n2, _ = s.shape
    n = n2 // 2
    return jnp.stack([s[:, :, :n, :n], s[:, :, n:, n:]], axis=2).reshape(b, 2 * hp, n, n)


def _trunk(x, s_gla, s_gdn, c_gdn, s_rw, c_rw, w):
    b, t, d = x.shape
    m = b * t
    tm = min(512, m)
    c = min(CHUNK, t)
    nc = t // c
    xf = x.reshape(m, d)
    h = _norm_call(xf, w["norm1"][0], tm)
    outs = ([], [], [], [], [])
    vf = None
    for l in range(DEPTH):
        zg = _mm_call(h, w["w_gla"][l], tm, ZG_W + ZS_W, "proj_gla")
        zd = _mm_call(h, w["w_gdn"][l], tm, ZD_W // 2, "proj_gdn")
        zr = _mm_call(h, w["w_rw"][l], tm, RW_PROJ // 2, "proj_rw")
        zs = zg
        prm = [w[n][l] for n in ("rw_mu", "rw_w0", "rw_wup", "rw_a0", "rw_aup", "rw_gup", "rw_kk", "rw_ka",
                                 "rw_rk", "rw_lng", "rw_lnb")]
        vmix = None if l == 0 else [w[n][l - 1] for n in ("rw_v0", "rw_vdown", "rw_vup")]
        parts = [
            _rw_part(zr, vf, vmix, prm, c_rw[l], _pair_block_diag(s_rw[l]), b, t, c, nc),
            _gdn_part(zd, zs, w["gdn_cw"][l], w["gdn_alog"][l], w["gdn_dtb"][l], w["gdn_gn"][l],
                      c_gdn[l], s_gdn[l], b, t, c, nc),
            _gla_part(zg, zs, w["gla_aup"][l], w["gla_abias"][l], w["gla_gn"][l],
                      s_gla[l].reshape(b, GLA_HEADS // 2, 2 * GLA_DK, GLA_DV), b, t, c, nc),
        ]
        res, (od, cg, sd), (og, sg) = _mixers_call(parts, b, nc)
        if l == 0:
            orw, vf, cr, sr = res
        else:
            orw, cr, sr = res
        flat = lambda a: a.reshape(m, a.shape[-1])
        xf, h2 = _outproj_call(flat(og), flat(od), flat(orw), w["w_out"][l], xf, w["norm2"][l], tm)
        last = l == DEPTH - 1
        g_next = w["final"] if last else w["norm1"][l + 1]
        res = _ffn_call(h2, w["w_g"][l], w["w_u"][l], w["w_d"][l], xf, g_next, tm, 512, last)
        if last:
            (y,) = res
        else:
            xf, h = res
        for lst, arr in zip(outs, (sg.reshape(b, GLA_HEADS, GLA_DK, GLA_DV), sd, cg, _pair_blocks(sr), cr)):
            lst.append(arr)
    return y.reshape(b, t, d), [jnp.stack(lst) for lst in outs]


def kernel(x_prompt, x_sample, state_gla, state_gdn, cache_gdn_conv, state_rwkv, cache_rwkv_shift, norm1_g, w_in, gla_a_up, gla_a_bias, gla_norm_g, gdn_conv_w, gdn_A_log, gdn_dt_bias, gdn_norm_g, rw_mu, rw_w0, rw_w_up, rw_a0, rw_a_up, rw_v0, rw_v_down, rw_v_up, rw_g_up, rw_k_k, rw_k_a, rw_r_k, rw_ln_g, rw_ln_b, w_out, norm2_g, w_ffn_gate, w_ffn_up, w_ffn_down, final_norm_g):
    p = dict(norm1_g=norm1_g, w_in=w_in, gla_a_up=gla_a_up, gla_a_bias=gla_a_bias, gla_norm_g=gla_norm_g,
             gdn_conv_w=gdn_conv_w, gdn_A_log=gdn_A_log, gdn_dt_bias=gdn_dt_bias, gdn_norm_g=gdn_norm_g,
             rw_mu=rw_mu, rw_w0=rw_w0, rw_w_up=rw_w_up, rw_a0=rw_a0, rw_a_up=rw_a_up, rw_v0=rw_v0,
             rw_v_down=rw_v_down, rw_v_up=rw_v_up, rw_g_up=rw_g_up, rw_k_k=rw_k_k, rw_k_a=rw_k_a,
             rw_r_k=rw_r_k, rw_ln_g=rw_ln_g, rw_ln_b=rw_ln_b, w_out=w_out, norm2_g=norm2_g,
             w_ffn_gate=w_ffn_gate, w_ffn_up=w_ffn_up, w_ffn_down=w_ffn_down, final_norm_g=final_norm_g)
    w = _prep_weights(p)
    bp = x_prompt.shape[0]
    zeros = lambda a: jnp.zeros((DEPTH, bp) + a.shape[2:], a.dtype)
    y_p, st_p = _trunk(x_prompt, zeros(state_gla), zeros(state_gdn), zeros(cache_gdn_conv), zeros(state_rwkv),
                       zeros(cache_rwkv_shift), w)
    y_s, st_s = _trunk(x_sample, state_gla, state_gdn, cache_gdn_conv, state_rwkv, cache_rwkv_shift, w)
    return (y_p, y_s, *st_p, *st_s)
```

```python
import functools

import numpy as np
import jax
import jax.numpy as jnp
from jax import lax
from jax.experimental import pallas as pl
from jax.experimental.pallas import tpu as pltpu

D_MODEL = 2048
DEPTH = 4
CHUNK = 64
NORM_EPS = 1e-6
L2_EPS = 1e-6
GLA_HEADS, GLA_DK, GLA_DV, GLA_LORA, GLA_GATE_TEMP = 4, 64, 128, 16, 16.0
GDN_HEADS, GDN_DK, GDN_DV, CONV_W = 6, 128, 128, 4
RW_HEADS, RW_N = 12, 64
RW_DECAY_LORA, RW_AAA_LORA, RW_MV_LORA, RW_GATE_LORA = 64, 64, 32, 128
RW_GN_EPS = 64e-5

GLA_QK = GLA_HEADS * GLA_DK
GLA_V = GLA_HEADS * GLA_DV
GDN_QK = GDN_HEADS * GDN_DK
GDN_V = GDN_HEADS * GDN_DV
GDN_CONV_CH = 2 * GDN_QK + GDN_V
RW_C = RW_HEADS * RW_N
GLA_PROJ = 2 * GLA_QK + GLA_V + GLA_LORA + GLA_V
GDN_PROJ = GDN_CONV_CH + 2 * GDN_HEADS + GDN_V
RW_PROJ = 3 * RW_C + RW_DECAY_LORA + RW_AAA_LORA + RW_GATE_LORA
D_FF = -(-8 * D_MODEL // (3 * 256)) * 256

LANES = 128
ZG_W = 2 * GLA_QK + 2 * GLA_V
ZD_W = GDN_CONV_CH + GDN_V
ZS_W = LANES
ZS_BETA = GLA_LORA
ZS_DT = GLA_LORA + GDN_HEADS
VMEM_LIMIT = 52 * 1024 * 1024

_BF = jnp.bfloat16
_F32 = jnp.float32
_NEG = -1e30


def _dot(a, b):
    return jnp.dot(a.astype(_BF), b.astype(_BF), preferred_element_type=_F32)


def _dot_nt(a, b):
    return lax.dot_general(a.astype(_BF), b.astype(_BF), (((1,), (1,)), ((), ())),
                           preferred_element_type=_F32)


def _dot_tn(a, b):
    return lax.dot_general(a.astype(_BF), b.astype(_BF), (((0,), (0,)), ((), ())),
                           preferred_element_type=_F32)


def _split2(x):
    hi = x.astype(_BF)
    lo = (x - hi.astype(_F32)).astype(_BF)
    return hi, lo


def _cdot(c, x):
    return sum(jnp.dot(c, p, preferred_element_type=_F32) for p in _split2(x))


def _cdot_tn(x, c):
    return sum(lax.dot_general(p, c, (((0,), (0,)), ((), ())), preferred_element_type=_F32)
               for p in _split2(x))


def _sigmoid(x):
    return 1.0 / (1.0 + jnp.exp(-x))


def _silu(x):
    return x * _sigmoid(x)


def _softplus(x):
    return jnp.maximum(x, 0.0) + jnp.log(1.0 + jnp.exp(-jnp.abs(x)))


HALF = LANES // 2


def _rows_at(x, first, total=LANES):
    n, w = x.shape
    parts = [jnp.zeros((first, w), x.dtype)] if first else []
    parts.append(x)
    if total - first - n:
        parts.append(jnp.zeros((total - first - n, w), x.dtype))
    return jnp.concatenate(parts, axis=0) if len(parts) > 1 else x


def _rows_pair(a, b):
    return jnp.concatenate([_rows_at(a, 0, HALF), _rows_at(b, 0, HALF)], axis=0)


def _inverse_start(lows, n):
    rows = lax.broadcasted_iota(jnp.int32, (n, LANES), 0)
    lane = lax.broadcasted_iota(jnp.int32, (n, LANES), 1)
    return [jnp.where(lane == rows + HALF, 1.0, 0.0) - low for low in lows]


def _inverse_step(ws, n):
    keep = lax.broadcasted_iota(jnp.int32, (n, LANES), 1) >= HALF
    wbs = [w.astype(_BF) for w in ws]
    ys = [jnp.dot(wb, _rows_at(wb, 0), preferred_element_type=_F32) for wb in wbs]
    return [y + jnp.where(keep, w, 0.0) for y, w in zip(ys, ws)]


def _solve_with(w, rhs):
    wb = w.astype(_BF)
    return sum(jnp.dot(wb, _rows_at(part, HALF), preferred_element_type=_F32) for part in _split2(rhs))


def _lane_halves(shape):
    lane = lax.broadcasted_iota(jnp.int32, shape, len(shape) - 1)
    return lane < (LANES // 2)


def _half_sum(x, half):
    s0 = jnp.sum(jnp.where(half, x, 0.0), axis=-1, keepdims=True)
    s1 = jnp.sum(jnp.where(half, 0.0, x), axis=-1, keepdims=True)
    return jnp.where(half, s0, s1)


def _rmsnorm_rows(x, g):
    return x * lax.rsqrt(jnp.mean(x * x, axis=-1, keepdims=True) + NORM_EPS) * g


def _params(*sem):
    return pltpu.CompilerParams(dimension_semantics=sem, vmem_limit_bytes=VMEM_LIMIT)


def _norm_kernel(x_ref, g_ref, o_ref):
    o_ref[...] = _rmsnorm_rows(x_ref[...], g_ref[...]).astype(o_ref.dtype)


def _norm_call(x, g, tm):
    m, d = x.shape
    return pl.pallas_call(
        _norm_kernel,
        grid=(m // tm,),
        in_specs=[pl.BlockSpec((tm, d), lambda i: (i, 0)), pl.BlockSpec((1, d), lambda i: (0, 0))],
        out_specs=pl.BlockSpec((tm, d), lambda i: (i, 0)),
        out_shape=jax.ShapeDtypeStruct((m, d), _BF),
        compiler_params=_params("arbitrary"),
        name="rmsnorm",
    )(x, g)


def _mm_kernel(a_ref, w_ref, o_ref):
    o_ref[...] = jnp.dot(a_ref[...], w_ref[...], preferred_element_type=_F32)


def _mm_call(a, w, l, tm, tn, name):
    m, k = a.shape
    n = w.shape[2]
    return pl.pallas_call(
        _mm_kernel,
        grid=(n // tn, m // tm),
        in_specs=[pl.BlockSpec((tm, k), lambda j, i: (i, 0)), pl.BlockSpec((None, k, tn), lambda j, i: (l, 0, j))],
        out_specs=pl.BlockSpec((tm, tn), lambda j, i: (i, j)),
        out_shape=jax.ShapeDtypeStruct((m, n), _F32),
        compiler_params=_params("arbitrary", "arbitrary"),
        name=name,
    )(a, w)


def _outproj_kernel(og_ref, od_ref, or_ref, w_ref, x_ref, g_ref, x1_ref, h_ref):
    d = functools.partial(jnp.dot, preferred_element_type=_F32)
    acc = (d(og_ref[...], w_ref[0:GLA_V, :]) + d(od_ref[...], w_ref[GLA_V:GLA_V + GDN_V, :])
           + d(or_ref[...], w_ref[GLA_V + GDN_V:, :]))
    x1 = x_ref[...] + acc
    x1_ref[...] = x1
    h_ref[...] = _rmsnorm_rows(x1, g_ref[...]).astype(h_ref.dtype)


def _outproj_call(og, od, orw, w, l, x, g, tm):
    m, d = x.shape
    row = lambda width: pl.BlockSpec((tm, width), lambda i: (i, 0))
    return pl.pallas_call(
        _outproj_kernel,
        grid=(m // tm,),
        in_specs=[row(GLA_V), row(GDN_V), row(RW_C), pl.BlockSpec((None,) + w.shape[1:], lambda i: (l, 0, 0)), row(d),
                  pl.BlockSpec((1, d), lambda i: (0, 0))],
        out_specs=[row(d), row(d)],
        out_shape=[jax.ShapeDtypeStruct((m, d), _F32), jax.ShapeDtypeStruct((m, d), _BF)],
        compiler_params=_params("arbitrary"),
        name="outproj",
    )(og, od, orw, w, x, g)


def _ffn_kernel(h_ref, wg_ref, wu_ref, wd_ref, x_ref, g_ref, *rest, nf, emit_x):
    if emit_x:
        x2_ref, hn_ref, acc_ref = rest
    else:
        hn_ref, acc_ref = rest
    f = pl.program_id(1)

    @pl.when(f == 0)
    def _():
        acc_ref[...] = jnp.zeros_like(acc_ref)

    h = h_ref[...]
    gate = jnp.dot(h, wg_ref[...], preferred_element_type=_F32)
    up = jnp.dot(h, wu_ref[...], preferred_element_type=_F32)
    act = (_silu(gate) * up).astype(_BF)
    acc_ref[...] += jnp.dot(act, wd_ref[...], preferred_element_type=_F32)

    @pl.when(f == nf - 1)
    def _():
        x2 = x_ref[...] + acc_ref[...]
        if emit_x:
            x2_ref[...] = x2
        hn_ref[...] = _rmsnorm_rows(x2, g_ref[...]).astype(hn_ref.dtype)


def _ffn_call(h, wg, wu, wd, l, x, g, tm, tf, last):
    m, d = x.shape
    nf = D_FF // tf
    row = pl.BlockSpec((tm, d), lambda i, f: (i, 0))
    if last:
        out_specs = [row]
        out_shape = [jax.ShapeDtypeStruct((m, d), _F32)]
    else:
        out_specs = [row, row]
        out_shape = [jax.ShapeDtypeStruct((m, d), _F32), jax.ShapeDtypeStruct((m, d), _BF)]
    return pl.pallas_call(
        functools.partial(_ffn_kernel, nf=nf, emit_x=not last),
        grid=(m // tm, nf),
        in_specs=[row, pl.BlockSpec((None, d, tf), lambda i, f: (l, 0, f)),
                  pl.BlockSpec((None, d, tf), lambda i, f: (l, 0, f)),
                  pl.BlockSpec((None, tf, d), lambda i, f: (l, f, 0)), row, pl.BlockSpec((1, d), lambda i, f: (0, 0))],
        out_specs=out_specs,
        out_shape=out_shape,
        scratch_shapes=[pltpu.VMEM((tm, d), _F32)],
        compiler_params=_params("arbitrary", "arbitrary"),
        name="ffn",
    )(h, wg, wu, wd, x, g)


def _gla_consts(c):
    nlev = int(np.log2(c))
    t = np.arange(c)
    blocks = [t[:, None] >= t[None, :],
              t[None, :] > t[:, None]]
    mq, mk, masks = [], [], []
    for lv in range(nlev):
        m = c >> lv
        half = m // 2
        blk, pos = t // m, t % m
        mid = blk * m + half
        upper = pos >= half
        mq.append(upper[:, None] & (t[None, :] >= mid[:, None]) & (t[None, :] <= t[:, None]))
        mk.append((~upper)[:, None] & (t[None, :] > t[:, None]) & (t[None, :] <= mid[:, None] - 1))
        masks.append((blk[:, None] == blk[None, :]) & upper[:, None] & (~upper)[None, :])
    mc = np.concatenate(blocks + mq + mk, axis=0).astype(np.float32)
    return jnp.asarray(mc, _BF), jnp.asarray(np.stack(masks).astype(np.float32)), nlev


def _gla_body(zg_ref, zs_ref, aup_ref, abias_ref, gn_ref, s0_ref, mc_ref, mk_ref,
              o_ref, sout_ref, s_scr, *, c, nlev, nc):
    step = pl.program_id(1)

    @pl.when(step == 0)
    def _():
        s_scr[...] = s0_ref[...]

    xa = _dot(zs_ref[...], aup_ref[...]) + abias_ref[...]
    la = (jnp.minimum(xa, 0.0) - jnp.log(1.0 + jnp.exp(-jnp.abs(xa)))) * (1.0 / GLA_GATE_TEMP)
    yield
    ex = jnp.exp(_cdot(mc_ref[...], la))
    yield
    q = zg_ref[:, 0:GLA_QK] * (GLA_DK ** -0.5)
    k = zg_ref[:, GLA_QK:2 * GLA_QK]
    qd = q * ex[0:c]
    kd = k * ex[c:2 * c]
    half = _lane_halves((c, LANES))
    eye = lax.broadcasted_iota(jnp.int32, (c, c), 0) == lax.broadcasted_iota(jnp.int32, (c, c), 1)
    ones = jnp.ones((c, LANES), _BF)
    heads = range(GLA_HEADS)
    sls = [slice(LANES * (h // 2), LANES * (h // 2 + 1)) for h in heads]
    sel = [(lambda x: jnp.where(half, x, 0.0)) if h % 2 == 0 else (lambda x: jnp.where(half, 0.0, x)) for h in heads]
    v = [zg_ref[:, 2 * GLA_QK + GLA_DV * h:2 * GLA_QK + GLA_DV * (h + 1)] for h in heads]
    s_old = [s_scr[p] for p in range(GLA_HEADS // 2)]
    a = [jnp.where(eye, _dot_nt(sel[h](q[:, sls[h]]), k[:, sls[h]]), 0.0) for h in heads]
    yield
    for lv in range(nlev):
        exq = ex[(2 + lv) * c:(3 + lv) * c]
        kx = k * ex[(2 + nlev + lv) * c:(3 + nlev + lv) * c]
        qx = q * exq
        a = [a[h] + mk_ref[lv] * _dot_nt(sel[h](qx[:, sls[h]]), kx[:, sls[h]]) for h in heads]
        yield
    o = [_dot(sel[h](qd[:, sls[h]]), s_old[h // 2]) + _dot(a[h], v[h]) for h in heads]
    yield
    ms = [jnp.mean(o[h] * o[h], axis=-1, keepdims=True) for h in heads]
    for p in range(GLA_HEADS // 2):
        sl = sls[2 * p]
        s_scr[p] = (jnp.exp(_cdot_tn(la[:, sl], ones)) * s_old[p]
                    + _dot_tn(sel[0](kd[:, sl]), v[2 * p]) + _dot_tn(sel[1](kd[:, sl]), v[2 * p + 1]))
    yield
    for h in heads:
        g_h = zg_ref[:, 2 * GLA_QK + GLA_V + GLA_DV * h:2 * GLA_QK + GLA_V + GLA_DV * (h + 1)]
        y = o[h] * lax.rsqrt(ms[h] + NORM_EPS) * gn_ref[...] * _silu(g_h)
        o_ref[:, GLA_DV * h:GLA_DV * (h + 1)] = y.astype(o_ref.dtype)

    @pl.when(step == nc - 1)
    def _():
        sout_ref[...] = s_scr[...]


ROWS = 1
ROW_LAG = 6
MIXER_ORDER = ("rw", "gdn", "gla")


class _Part:
    def __init__(self, body, args, in_specs, out_specs, out_shape, scratch):
        self.body, self.args = body, list(args)
        self.in_specs, self.per_row = [s for s, _ in in_specs], [r for _, r in in_specs]
        self.out_specs, self.out_shape = [s for s, _ in out_specs], list(out_shape)
        self.scratch = [pltpu.VMEM((ROWS,) + shape, dtype) for shape, dtype in scratch]


def _full(a):
    return pl.BlockSpec(a.shape, lambda i, j: (0,) * a.ndim), False


def _per_batch(a):
    return pl.BlockSpec((ROWS,) + a.shape[1:], lambda i, j: (i,) + (0,) * (a.ndim - 1)), True


def _tok(c, w, col=0):
    return pl.BlockSpec((ROWS, c, w), lambda i, j: (i, j, col)), True


ZS_COL = ZG_W // ZS_W


def _gla_part(zg, zs, aup, abias, gn, s0, b, t, c, nc):
    mc, masks, nlev = _gla_consts(c)
    return _Part(
        functools.partial(_gla_body, c=c, nlev=nlev, nc=nc),
        [zg.reshape(b, t, -1), zs.reshape(b, t, -1), aup, abias, gn, s0, mc, masks],
        [_tok(c, ZG_W), _tok(c, ZS_W, ZS_COL), _full(aup), _full(abias), _full(gn), _per_batch(s0), _full(mc),
         _full(masks)],
        [_tok(c, GLA_V), _per_batch(s0)],
        [jax.ShapeDtypeStruct((b, t, GLA_V), _BF), jax.ShapeDtypeStruct(s0.shape, _F32)],
        [(s0.shape[1:], _F32)])


def _mixers_kernel(*refs, bodies, counts, per_row):
    n_in, n_out = sum(x[0] for x in counts), sum(x[1] for x in counts)
    rows = []
    for r in range(ROWS):
        i0, o0, s0 = 0, n_in, n_in + n_out
        gens = []
        for body, (ni, no, ns), flags in zip(bodies, counts, per_row):
            ins = [ref.at[r] if f else ref for ref, f in zip(refs[i0:i0 + ni], flags)]
            rest = [ref.at[r] for ref in refs[o0:o0 + no] + refs[s0:s0 + ns]]
            gens.append(body(*ins, *rest))
            i0, o0, s0 = i0 + ni, o0 + no, s0 + ns
        rows.append(gens)
    active, rounds = [], 0
    while active or rows:
        if rows and rounds % ROW_LAG == 0:
            active += rows.pop(0)
        active = [g for g in active if next(g, StopIteration) is not StopIteration]
        rounds += 1


def _mixers_call(parts, b, nc):
    outs = pl.pallas_call(
        functools.partial(_mixers_kernel, bodies=[p.body for p in parts],
                          counts=[(len(p.args), len(p.out_shape), len(p.scratch)) for p in parts],
                          per_row=[p.per_row for p in parts]),
        grid=(b // ROWS, nc),
        in_specs=[s for p in parts for s in p.in_specs],
        out_specs=[s for p in parts for s in p.out_specs],
        out_shape=[s for p in parts for s in p.out_shape],
        scratch_shapes=[s for p in parts for s in p.scratch],
        compiler_params=_params("arbitrary", "arbitrary"),
        name="mixers",
    )(*[a for p in parts for a in p.args])
    res, k = [], 0
    for p in parts:
        res.append(outs[k:k + len(p.out_shape)])
        k += len(p.out_shape)
    return res


def _gdn_body(zd_ref, zs_ref, cw_ref, alog_ref, dtb_ref, gn_ref, cache_ref, s0_ref, tril_ref, triu_ref, mlast_ref,
              o_ref, cout_ref, sout_ref, cbuf, s_scr, *, c, nc):
    step = pl.program_id(1)
    tail0 = 8 - (CONV_W - 1)

    @pl.when(step == 0)
    def _():
        cbuf[tail0:8, :] = cache_ref[...]
        s_scr[...] = s0_ref[...]

    x = zd_ref[:, 0:GDN_CONV_CH]
    cbuf[8:8 + c, :] = x
    y = cbuf[tail0:tail0 + c, :] * cw_ref[0:1, :]
    for j in range(1, CONV_W - 1):
        y = y + cbuf[tail0 + j:tail0 + j + c, :] * cw_ref[j:j + 1, :]
    y = y + x * cw_ref[CONV_W - 1:CONV_W, :]
    tail = cbuf[c + tail0:c + 8, :]
    cbuf[tail0:8, :] = tail

    @pl.when(step == nc - 1)
    def _():
        cout_ref[...] = tail

    yield
    ys = _silu(y)
    yield
    zs = zs_ref[...]
    lg = -jnp.exp(alog_ref[...]) * _softplus(zs + dtb_ref[...])
    bt = _sigmoid(zs)
    bcum = _cdot(tril_ref[...], lg)
    bcum_t = _cdot_tn(lg, triu_ref[...])
    bdl = _cdot(mlast_ref[...], lg)
    btot_t = _cdot_tn(lg, jnp.ones((c, LANES), _BF))
    lane = lax.broadcasted_iota(jnp.int32, (c, LANES), 1)
    rows = lax.broadcasted_iota(jnp.int32, (c, LANES), 0)
    causal = (rows >= lane) & (lane < c)
    strict = (rows > lane) & (lane < c)
    heads = range(GDN_HEADS)
    pick = lambda tile, j: jnp.sum(jnp.where(lane == j, tile, 0.0), axis=1, keepdims=True)
    bcol = [pick(bcum, ZS_DT + h) for h in heads]
    beta = [pick(bt, ZS_BETA + h) for h in heads]
    dlcol = [pick(bdl, ZS_DT + h) for h in heads]
    dec = [jnp.exp(jnp.where(causal, bcol[h] - bcum_t[ZS_DT + h:ZS_DT + h + 1, :], _NEG)) for h in heads]
    eb = [jnp.exp(bcol[h]) for h in heads]
    yield
    q, k, v = [], [], []
    for h in heads:
        q_h = ys[:, GDN_DK * h:GDN_DK * (h + 1)]
        k_h = ys[:, GDN_QK + GDN_DK * h:GDN_QK + GDN_DK * (h + 1)]
        q.append(q_h * lax.rsqrt(jnp.sum(q_h * q_h, axis=-1, keepdims=True) + L2_EPS) * (GDN_DK ** -0.5))
        k.append(k_h * lax.rsqrt(jnp.sum(k_h * k_h, axis=-1, keepdims=True) + L2_EPS))
        v.append(ys[:, 2 * GDN_QK + GDN_DV * h:2 * GDN_QK + GDN_DV * (h + 1)])
    qk = [jnp.concatenate([q[h], k[h]], axis=0) for h in heads]
    s_old = [s_scr[h] for h in heads]
    yield
    sc = [_dot_nt(qk[h], _rows_at(k[h], 0)) for h in heads]
    ps = [_dot(qk[h], s_old[h]) for h in heads]
    yield
    ws = _inverse_start([jnp.where(strict, beta[h] * dec[h] * sc[h][c:], 0.0) for h in heads], c)
    for _ in range(int(np.log2(c))):
        ws = _inverse_step(ws, c)
        yield
    delta = [_solve_with(ws[h], beta[h] * (v[h] - eb[h] * ps[h][c:])) for h in heads]
    yield
    o = [eb[h] * ps[h][:c] + _dot(sc[h][:c] * dec[h], _rows_at(delta[h], 0)) for h in heads]
    yield
    ms = [jnp.mean(o[h] * o[h], axis=-1, keepdims=True) for h in heads]
    for h in heads:
        elast = jnp.exp(btot_t[ZS_DT + h:ZS_DT + h + 1, :])
        s_scr[h] = elast * s_old[h] + _dot_tn(k[h] * jnp.exp(dlcol[h]), delta[h])
    yield
    for h in heads:
        g_h = zd_ref[:, GDN_CONV_CH + GDN_DV * h:GDN_CONV_CH + GDN_DV * (h + 1)]
        y_h = o[h] * lax.rsqrt(ms[h] + NORM_EPS) * gn_ref[...] * _silu(g_h)
        o_ref[:, GDN_DV * h:GDN_DV * (h + 1)] = y_h.astype(o_ref.dtype)

    @pl.when(step == nc - 1)
    def _():
        sout_ref[...] = s_scr[...]


def _tri_consts(c):
    t = np.arange(c)
    tril = (t[:, None] >= t[None, :]).astype(np.float32)
    mlast = (t[None, :] > t[:, None]).astype(np.float32)
    triu_wide = np.pad(tril.T, ((0, 0), (0, LANES - c)))
    return jnp.asarray(tril, _BF), jnp.asarray(triu_wide, _BF), jnp.asarray(mlast, _BF)


def _gdn_part(zd, zs, cw, alog, dtb, gn, cache, s0, b, t, c, nc):
    tril, triu, mlast = _tri_consts(c)
    return _Part(
        functools.partial(_gdn_body, c=c, nc=nc),
        [zd.reshape(b, t, -1), zs.reshape(b, t, -1), cw, alog, dtb, gn, cache, s0, tril, triu, mlast],
        [_tok(c, ZD_W), _tok(c, ZS_W, ZS_COL), _full(cw), _full(alog), _full(dtb), _full(gn), _per_batch(cache),
         _per_batch(s0), _full(tril), _full(triu), _full(mlast)],
        [_tok(c, GDN_V), _per_batch(cache), _per_batch(s0)],
        [jax.ShapeDtypeStruct((b, t, GDN_V), _BF), jax.ShapeDtypeStruct(cache.shape, _F32),
         jax.ShapeDtypeStruct(s0.shape, _F32)],
        [((c + 8, GDN_CONV_CH), _F32), (s0.shape[1:], _F32)])


def _rw_body(*refs, c, nc, first):
    if first:
        (zr_ref, mu_ref, w0_ref, wup_ref, a0_ref, aup_ref, gup_ref, kk_ref, ka_ref, rk_ref, lng_ref, lnb_ref,
         cache_ref, s0_ref, tril_ref, mlast_ref, o_ref, vf_out_ref, shout_ref, sout_ref, sbuf, s_scr) = refs
    else:
        (zr_ref, vf_ref, v0_ref, vdown_ref, vup_ref, mu_ref, w0_ref, wup_ref, a0_ref, aup_ref, gup_ref, kk_ref,
         ka_ref, rk_ref, lng_ref, lnb_ref, cache_ref, s0_ref, tril_ref, mlast_ref, o_ref, shout_ref, sout_ref,
         sbuf, s_scr) = refs
    step = pl.program_id(1)

    @pl.when(step == 0)
    def _():
        sbuf[7:8, :] = cache_ref[...]
        s_scr[...] = s0_ref[...]

    x = zr_ref[...]
    sbuf[8:8 + c, :] = x
    zprev = sbuf[7:7 + c, :]
    last = sbuf[c + 7:c + 8, :]
    sbuf[7:8, :] = last

    @pl.when(step == nc - 1)
    def _():
        shout_ref[...] = last

    yield
    zm = x + (zprev - x) * mu_ref[...]
    xr, xk, xv = zm[:, 0:RW_C], zm[:, RW_C:2 * RW_C], zm[:, 2 * RW_C:3 * RW_C]
    xwa = zm[:, 3 * RW_C:3 * RW_C + LANES]
    xg = zm[:, 3 * RW_C + LANES:]
    wlog = -_softplus(-(w0_ref[...] + _dot(jnp.tanh(xwa), wup_ref[...]))) - 0.5
    lw = -jnp.exp(wlog)
    a = _sigmoid(a0_ref[...] + _dot(xwa, aup_ref[...]))
    yield
    if first:
        vf_out_ref[...] = xv
    else:
        nu = _sigmoid(v0_ref[...] + _dot(_dot(xv, vdown_ref[...]), vup_ref[...]))
        xv = xv + (vf_ref[...] - xv) * nu
    kkp = xk * kk_ref[...]
    xk2 = xk * (1.0 + (a - 1.0) * ka_ref[...])
    gate = _dot(_sigmoid(xg), gup_ref[...])
    cum = _cdot(tril_ref[...], lw)
    yield
    e_c = jnp.exp(cum)
    e_cp = jnp.exp(cum - lw)
    e_nc = jnp.exp(-cum)
    e_dl = jnp.exp(_cdot(mlast_ref[...], lw))
    half = _lane_halves((c, LANES))
    half2 = _lane_halves((2 * c, LANES))
    rows = lax.broadcasted_iota(jnp.int32, (c, LANES), 0)
    cols = lax.broadcasted_iota(jnp.int32, (c, LANES), 1) & (HALF - 1)
    strict = (rows > cols) & (cols < c)
    causal = (rows >= cols) & (cols < c)
    srow = lax.broadcasted_iota(jnp.int32, (LANES, LANES), 0) < (LANES // 2)
    scol = lax.broadcasted_iota(jnp.int32, (LANES, LANES), 1) < (LANES // 2)
    same_head = srow == scol
    ones = jnp.ones((c, LANES), _BF)
    pairs = range(RW_HEADS // 2)
    sls = [slice(LANES * p, LANES * (p + 1)) for p in pairs]
    r, k, v = [xr[:, s] for s in sls], [xk2[:, s] for s in sls], [xv[:, s] for s in sls]
    kap = [kkp[:, s] for s in sls]
    kap = [x * lax.rsqrt(_half_sum(x * x, half) + L2_EPS) for x in kap]
    ahat = [-(kap[p] * a[:, sls[p]]) for p in pairs]
    x2 = [jnp.concatenate([kap[p] * e_cp[:, sls[p]], r[p] * e_c[:, sls[p]]], axis=0) for p in pairs]
    at = [ahat[p] * e_nc[:, sls[p]] for p in pairs]
    kt = [k[p] * e_nc[:, sls[p]] for p in pairs]
    s_bd = [s_scr[p] for p in pairs]
    bsum = [_half_sum(r[p] * k[p] * rk_ref[:, sls[p]], half) for p in pairs]
    yield
    ps = [_dot(x2[p], s_bd[p]) for p in pairs]
    sc = []
    for p in pairs:
        sides = _rows_pair(at[p], kt[p])
        x2b = x2[p].astype(_BF)
        for xm in (jnp.where(half2, x2b, 0), jnp.where(half2, 0, x2b)):
            sc.append(_dot_nt(xm, sides))
    yield
    ws = _inverse_start([jnp.where(half & strict, -x[:c], 0.0) for x in sc], c)
    akv = [_dot(jnp.where(half, 0.0, jnp.where(strict, sc[i][:c], 0.0)), _rows_at(v[i // 2], HALF))
           for i in range(RW_HEADS)]
    rhs = [ps[p][:c] + jnp.where(half, akv[2 * p], akv[2 * p + 1]) for p in pairs]
    for _ in range(int(np.log2(c))):
        ws = _inverse_step(ws, c)
        yield
    u = [jnp.where(half, _solve_with(ws[2 * p], rhs[p]), _solve_with(ws[2 * p + 1], rhs[p])) for p in pairs]
    yield
    uv = [_rows_pair(u[p], v[p]) for p in pairs]
    oh = [_dot(jnp.where(causal, sc[i][c:], 0.0), uv[i // 2]) for i in range(RW_HEADS)]
    o = [ps[p][c:] + jnp.where(half, oh[2 * p], oh[2 * p + 1]) for p in pairs]
    yield
    mean = [_half_sum(o[p], half) * (1.0 / RW_N) for p in pairs]
    yield
    for p in pairs:
        dl = e_dl[:, sls[p]]
        upd = _dot_tn(jnp.concatenate([ahat[p] * dl, k[p] * dl], axis=0), jnp.concatenate([u[p], v[p]], axis=0))
        s_scr[p] = jnp.exp(_cdot_tn(lw[:, sls[p]], ones)) * s_bd[p] + jnp.where(same_head, upd, 0.0)
    dev = [o[p] - mean[p] for p in pairs]
    var = [_half_sum(dev[p] * dev[p], half) * (1.0 / RW_N) for p in pairs]
    yield
    for p in pairs:
        on = dev[p] * lax.rsqrt(var[p] + RW_GN_EPS) * lng_ref[:, sls[p]] + lnb_ref[:, sls[p]]
        o_ref[:, sls[p]] = ((on + bsum[p] * v[p]) * gate[:, sls[p]]).astype(o_ref.dtype)

    @pl.when(step == nc - 1)
    def _():
        sout_ref[...] = s_scr[...]


def _rw_part(zr, vf, vmix, prm, cache, s0, b, t, c, nc):
    tril, _, mlast = _tri_consts(c)
    first = vf is None
    args = ([zr.reshape(b, t, -1)] + ([] if first else [vf.reshape(b, t, -1)] + list(vmix)) + list(prm)
            + [cache, s0, tril, mlast])
    in_specs = ([_tok(c, RW_PROJ)] + ([] if first else [_tok(c, RW_C)] + [_full(a) for a in vmix])
                + [_full(a) for a in prm] + [_per_batch(cache), _per_batch(s0), _full(tril), _full(mlast)])
    out_specs = [_tok(c, RW_C)] + ([_tok(c, RW_C)] if first else []) + [_per_batch(cache), _per_batch(s0)]
    out_shape = ([jax.ShapeDtypeStruct((b, t, RW_C), _BF)] + ([jax.ShapeDtypeStruct((b, t, RW_C), _F32)] if first else [])
                 + [jax.ShapeDtypeStruct(cache.shape, _F32), jax.ShapeDtypeStruct(s0.shape, _F32)])
    return _Part(functools.partial(_rw_body, c=c, nc=nc, first=first), args, in_specs, out_specs, out_shape,
                 [((c + 8, RW_PROJ), _F32), (s0.shape[1:], _F32)])


def _prep_weights(p):
    bf = lambda a: a.astype(_BF)
    w_in = bf(p["w_in"])
    o1 = GLA_PROJ
    o2 = GLA_PROJ + GDN_PROJ
    lora0 = 2 * GLA_QK + GLA_V
    narrow = jnp.concatenate([w_in[:, :, lora0:lora0 + GLA_LORA],
                              w_in[:, :, o1 + GDN_CONV_CH:o1 + GDN_CONV_CH + 2 * GDN_HEADS]], axis=2)
    w_small = jnp.pad(narrow, ((0, 0), (0, 0), (0, ZS_W - narrow.shape[2])))
    w_gla = jnp.concatenate([w_in[:, :, 0:lora0], w_in[:, :, lora0 + GLA_LORA:o1], w_small], axis=2)
    w_gdn = jnp.concatenate([w_in[:, :, o1:o1 + GDN_CONV_CH], w_in[:, :, o1 + GDN_CONV_CH + 2 * GDN_HEADS:o2]], axis=2)
    w_rw = w_in[:, :, o2:]
    pad_rows = lambda a, lo, hi: jnp.pad(a, ((0, 0), (lo, hi), (0, 0)))
    pad_lanes = lambda a, lo: jnp.pad(a, ((0, 0), (lo, ZS_W - lo - a.shape[1])))[:, None, :]
    row = lambda a: a.reshape(a.shape[0], 1, -1)
    return dict(
        w_gla=w_gla, w_gdn=w_gdn, w_rw=w_rw,
        w_out=bf(p["w_out"]), w_g=bf(p["w_ffn_gate"]), w_u=bf(p["w_ffn_up"]), w_d=bf(p["w_ffn_down"]),
        gla_aup=bf(pad_rows(p["gla_a_up"], 0, ZS_W - GLA_LORA)), gla_abias=row(p["gla_a_bias"]),
        gla_gn=row(p["gla_norm_g"]),
        gdn_cw=p["gdn_conv_w"], gdn_alog=pad_lanes(p["gdn_A_log"], ZS_DT), gdn_dtb=pad_lanes(p["gdn_dt_bias"], ZS_DT),
        gdn_gn=row(p["gdn_norm_g"]),
        rw_mu=row(p["rw_mu"]), rw_w0=row(p["rw_w0"]), rw_wup=bf(pad_rows(p["rw_w_up"], 0, LANES - RW_DECAY_LORA)),
        rw_a0=row(p["rw_a0"]), rw_aup=bf(pad_rows(p["rw_a_up"], RW_DECAY_LORA, 0)), rw_gup=bf(p["rw_g_up"]),
        rw_kk=row(p["rw_k_k"]), rw_ka=row(p["rw_k_a"]), rw_rk=p["rw_r_k"].reshape(DEPTH, 1, RW_C),
        rw_lng=row(p["rw_ln_g"]), rw_lnb=row(p["rw_ln_b"]),
        rw_v0=row(p["rw_v0"]), rw_vdown=bf(p["rw_v_down"]), rw_vup=bf(p["rw_v_up"]),
        norm1=row(p["norm1_g"]), norm2=row(p["norm2_g"]), final=p["final_norm_g"].reshape(1, -1),
    )


def _pair_block_diag(s):
    b, h, n, _ = s.shape
    s = s.reshape(b, h // 2, 2, n, n)
    z = jnp.zeros_like(s[:, :, 0])
    top = jnp.concatenate([s[:, :, 0], z], axis=-1)
    bot = jnp.concatenate([z, s[:, :, 1]], axis=-1)
    return jnp.concatenate([top, bot], axis=-2)


def _pair_blocks(s):
    b, hp, n2, _ = s.shape
    n = n2 // 2
    return jnp.stack([s[:, :, :n, :n], s[:, :, n:, n:]], axis=2).reshape(b, 2 * hp, n, n)


def _trunk(x, s_gla, s_gdn, c_gdn, s_rw, c_rw, w):
    b, t, d = x.shape
    m = b * t
    tm = min(512, m)
    tp = min(1024, m)
    c = min(CHUNK, t)
    nc = t // c
    xf = x.reshape(m, d)
    h = _norm_call(xf, w["norm1"][0], tm)
    outs = ([], [], [], [], [])
    vf = None
    for l in range(DEPTH):
        zg = _mm_call(h, w["w_gla"], l, tp, ZG_W + ZS_W, "proj_gla")
        zd = _mm_call(h, w["w_gdn"], l, tp, ZD_W // 2, "proj_gdn")
        zr = _mm_call(h, w["w_rw"], l, tp, RW_PROJ // 2, "proj_rw")
        zs = zg
        prm = [w[n][l] for n in ("rw_mu", "rw_w0", "rw_wup", "rw_a0", "rw_aup", "rw_gup", "rw_kk", "rw_ka",
                                 "rw_rk", "rw_lng", "rw_lnb")]
        vmix = None if l == 0 else [w[n][l - 1] for n in ("rw_v0", "rw_vdown", "rw_vup")]
        parts = dict(
            rw=_rw_part(zr, vf, vmix, prm, c_rw[l], _pair_block_diag(s_rw[l]), b, t, c, nc),
            gdn=_gdn_part(zd, zs, w["gdn_cw"][l], w["gdn_alog"][l], w["gdn_dtb"][l], w["gdn_gn"][l],
                          c_gdn[l], s_gdn[l], b, t, c, nc),
            gla=_gla_part(zg, zs, w["gla_aup"][l], w["gla_abias"][l], w["gla_gn"][l],
                          s_gla[l].reshape(b, GLA_HEADS // 2, 2 * GLA_DK, GLA_DV), b, t, c, nc))
        outs_by = dict(zip(MIXER_ORDER, _mixers_call([parts[n] for n in MIXER_ORDER], b, nc)))
        res, (od, cg, sd), (og, sg) = outs_by["rw"], outs_by["gdn"], outs_by["gla"]
        if l == 0:
            orw, vf, cr, sr = res
        else:
            orw, cr, sr = res
        flat = lambda a: a.reshape(m, a.shape[-1])
        xf, h2 = _outproj_call(flat(og), flat(od), flat(orw), w["w_out"], l, xf, w["norm2"][l], tm)
        last = l == DEPTH - 1
        g_next = w["final"] if last else w["norm1"][l + 1]
        res = _ffn_call(h2, w["w_g"], w["w_u"], w["w_d"], l, xf, g_next, tm, 512, last)
        if last:
            (y,) = res
        else:
            xf, h = res
        for lst, arr in zip(outs, (sg.reshape(b, GLA_HEADS, GLA_DK, GLA_DV), sd, cg, _pair_blocks(sr), cr)):
            lst.append(arr)
    return y.reshape(b, t, d), [jnp.stack(lst) for lst in outs]


def kernel(x_prompt, x_sample, state_gla, state_gdn, cache_gdn_conv, state_rwkv, cache_rwkv_shift, norm1_g, w_in, gla_a_up, gla_a_bias, gla_norm_g, gdn_conv_w, gdn_A_log, gdn_dt_bias, gdn_norm_g, rw_mu, rw_w0, rw_w_up, rw_a0, rw_a_up, rw_v0, rw_v_down, rw_v_up, rw_g_up, rw_k_k, rw_k_a, rw_r_k, rw_ln_g, rw_ln_b, w_out, norm2_g, w_ffn_gate, w_ffn_up, w_ffn_down, final_norm_g):
    p = dict(norm1_g=norm1_g, w_in=w_in, gla_a_up=gla_a_up, gla_a_bias=gla_a_bias, gla_norm_g=gla_norm_g,
             gdn_conv_w=gdn_conv_w, gdn_A_log=gdn_A_log, gdn_dt_bias=gdn_dt_bias, gdn_norm_g=gdn_norm_g,
             rw_mu=rw_mu, rw_w0=rw_w0, rw_w_up=rw_w_up, rw_a0=rw_a0, rw_a_up=rw_a_up, rw_v0=rw_v0,
             rw_v_down=rw_v_down, rw_v_up=rw_v_up, rw_g_up=rw_g_up, rw_k_k=rw_k_k, rw_k_a=rw_k_a,
             rw_r_k=rw_r_k, rw_ln_g=rw_ln_g, rw_ln_b=rw_ln_b, w_out=w_out, norm2_g=norm2_g,
             w_ffn_gate=w_ffn_gate, w_ffn_up=w_ffn_up, w_ffn_down=w_ffn_down, final_norm_g=final_norm_g)
    w = _prep_weights(p)
    bp = x_prompt.shape[0]
    zeros = lambda a: jnp.zeros((DEPTH, bp) + a.shape[2:], a.dtype)
    y_p, st_p = _trunk(x_prompt, zeros(state_gla), zeros(state_gdn), zeros(cache_gdn_conv), zeros(state_rwkv),
                       zeros(cache_rwkv_shift), w)
    y_s, st_s = _trunk(x_sample, state_gla, state_gdn, cache_gdn_conv, state_rwkv, cache_rwkv_shift, w)
    return (y_p, y_s, *st_p, *st_s)
```

```python
import functools

import numpy as np
import jax
import jax.numpy as jnp
from jax import lax
from jax.experimental import pallas as pl
from jax.experimental.pallas import tpu as pltpu

D_MODEL = 2048
DEPTH = 4
CHUNK = 64
NORM_EPS = 1e-6
L2_EPS = 1e-6
GLA_HEADS, GLA_DK, GLA_DV, GLA_LORA, GLA_GATE_TEMP = 4, 64, 128, 16, 16.0
GDN_HEADS, GDN_DK, GDN_DV, CONV_W = 6, 128, 128, 4
RW_HEADS, RW_N = 12, 64
RW_DECAY_LORA, RW_AAA_LORA, RW_MV_LORA, RW_GATE_LORA = 64, 64, 32, 128
RW_GN_EPS = 64e-5

GLA_QK = GLA_HEADS * GLA_DK
GLA_V = GLA_HEADS * GLA_DV
GDN_QK = GDN_HEADS * GDN_DK
GDN_V = GDN_HEADS * GDN_DV
GDN_CONV_CH = 2 * GDN_QK + GDN_V
RW_C = RW_HEADS * RW_N
GLA_PROJ = 2 * GLA_QK + GLA_V + GLA_LORA + GLA_V
GDN_PROJ = GDN_CONV_CH + 2 * GDN_HEADS + GDN_V
RW_PROJ = 3 * RW_C + RW_DECAY_LORA + RW_AAA_LORA + RW_GATE_LORA
D_FF = -(-8 * D_MODEL // (3 * 256)) * 256

LANES = 128
ZG_W = 2 * GLA_QK + 2 * GLA_V
ZD_W = GDN_CONV_CH + GDN_V
ZS_W = LANES
ZS_BETA = GLA_LORA
ZS_DT = GLA_LORA + GDN_HEADS
VMEM_LIMIT = 52 * 1024 * 1024

_BF = jnp.bfloat16
_F32 = jnp.float32
_NEG = -1e30


def _dot(a, b):
    return jnp.dot(a.astype(_BF), b.astype(_BF), preferred_element_type=_F32)


def _dot_nt(a, b):
    return lax.dot_general(a.astype(_BF), b.astype(_BF), (((1,), (1,)), ((), ())),
                           preferred_element_type=_F32)


def _dot_tn(a, b):
    return lax.dot_general(a.astype(_BF), b.astype(_BF), (((0,), (0,)), ((), ())),
                           preferred_element_type=_F32)


def _split2(x):
    hi = x.astype(_BF)
    lo = (x - hi.astype(_F32)).astype(_BF)
    return hi, lo


def _cdot(c, x):
    hi, lo = _split2(x)
    n = x.shape[1]
    if n > LANES:
        return jnp.dot(c, hi, preferred_element_type=_F32) + jnp.dot(c, lo, preferred_element_type=_F32)
    y = jnp.dot(c, jnp.concatenate([hi, lo], axis=1), preferred_element_type=_F32)
    return y[:, :n] + y[:, n:]


def _cdot_tn(x, c):
    n = x.shape[1]
    y = lax.dot_general(jnp.concatenate(_split2(x), axis=1), c, (((0,), (0,)), ((), ())),
                        preferred_element_type=_F32)
    return y[:n] + y[n:]


def _sigmoid(x):
    return 1.0 / (1.0 + jnp.exp(-x))


def _silu(x):
    return x * _sigmoid(x)


def _softplus(x):
    return jnp.maximum(x, 0.0) + jnp.log(1.0 + jnp.exp(-jnp.abs(x)))


HALF = LANES // 2


def _rows_at(x, first, total=LANES):
    n, w = x.shape
    parts = [jnp.zeros((first, w), x.dtype)] if first else []
    parts.append(x)
    if total - first - n:
        parts.append(jnp.zeros((total - first - n, w), x.dtype))
    return jnp.concatenate(parts, axis=0) if len(parts) > 1 else x


def _rows_pair(a, b):
    return jnp.concatenate([_rows_at(a, 0, HALF), _rows_at(b, 0, HALF)], axis=0)


def _inverse_start(lows, n):
    rows = lax.broadcasted_iota(jnp.int32, (n, LANES), 0)
    lane = lax.broadcasted_iota(jnp.int32, (n, LANES), 1)
    return [jnp.where(lane == rows + HALF, 1.0, 0.0) - low for low in lows]


def _inverse_step(ws, n):
    keep = lax.broadcasted_iota(jnp.int32, (n, LANES), 1) >= HALF
    wbs = [w.astype(_BF) for w in ws]
    ys = [jnp.dot(wb, _rows_at(wb, 0), preferred_element_type=_F32) for wb in wbs]
    return [y + jnp.where(keep, w, 0.0) for y, w in zip(ys, ws)]


def _solve_with(w, rhs):
    n = rhs.shape[1]
    both = jnp.concatenate([_rows_at(part, HALF) for part in _split2(rhs)], axis=1)
    y = jnp.dot(w.astype(_BF), both, preferred_element_type=_F32)
    return y[:, :n] + y[:, n:]


def _lane_halves(shape):
    lane = lax.broadcasted_iota(jnp.int32, shape, len(shape) - 1)
    return lane < (LANES // 2)


def _half_sum(x, half):
    s0 = jnp.sum(jnp.where(half, x, 0.0), axis=-1, keepdims=True)
    s1 = jnp.sum(jnp.where(half, 0.0, x), axis=-1, keepdims=True)
    return jnp.where(half, s0, s1)


def _rmsnorm_rows(x, g):
    return x * lax.rsqrt(jnp.mean(x * x, axis=-1, keepdims=True) + NORM_EPS) * g


def _params(*sem):
    return pltpu.CompilerParams(dimension_semantics=sem, vmem_limit_bytes=VMEM_LIMIT)


def _norm_kernel(x_ref, g_ref, o_ref):
    o_ref[...] = _rmsnorm_rows(x_ref[...], g_ref[...]).astype(o_ref.dtype)


def _norm_call(x, g, tm):
    m, d = x.shape
    return pl.pallas_call(
        _norm_kernel,
        grid=(m // tm,),
        in_specs=[pl.BlockSpec((tm, d), lambda i: (i, 0)), pl.BlockSpec((1, d), lambda i: (0, 0))],
        out_specs=pl.BlockSpec((tm, d), lambda i: (i, 0)),
        out_shape=jax.ShapeDtypeStruct((m, d), _BF),
        compiler_params=_params("arbitrary"),
        name="rmsnorm",
    )(x, g)


def _mm_kernel(a_ref, w_ref, o_ref):
    o_ref[...] = jnp.dot(a_ref[...], w_ref[...], preferred_element_type=_F32)


def _mm_call(a, w, l, tm, tn, name):
    m, k = a.shape
    n = w.shape[2]
    return pl.pallas_call(
        _mm_kernel,
        grid=(n // tn, m // tm),
        in_specs=[pl.BlockSpec((tm, k), lambda j, i: (i, 0)), pl.BlockSpec((None, k, tn), lambda j, i: (l, 0, j))],
        out_specs=pl.BlockSpec((tm, tn), lambda j, i: (i, j)),
        out_shape=jax.ShapeDtypeStruct((m, n), _F32),
        compiler_params=_params("arbitrary", "arbitrary"),
        name=name,
    )(a, w)


def _outproj_kernel(og_ref, od_ref, or_ref, w_ref, x_ref, g_ref, x1_ref, h_ref):
    d = functools.partial(jnp.dot, preferred_element_type=_F32)
    acc = (d(og_ref[...], w_ref[0:GLA_V, :]) + d(od_ref[...], w_ref[GLA_V:GLA_V + GDN_V, :])
           + d(or_ref[...], w_ref[GLA_V + GDN_V:, :]))
    x1 = x_ref[...] + acc
    x1_ref[...] = x1
    h_ref[...] = _rmsnorm_rows(x1, g_ref[...]).astype(h_ref.dtype)


def _outproj_call(og, od, orw, w, l, x, g, tm):
    m, d = x.shape
    row = lambda width: pl.BlockSpec((tm, width), lambda i: (i, 0))
    return pl.pallas_call(
        _outproj_kernel,
        grid=(m // tm,),
        in_specs=[row(GLA_V), row(GDN_V), row(RW_C), pl.BlockSpec((None,) + w.shape[1:], lambda i: (l, 0, 0)), row(d),
                  pl.BlockSpec((1, d), lambda i: (0, 0))],
        out_specs=[row(d), row(d)],
        out_shape=[jax.ShapeDtypeStruct((m, d), _F32), jax.ShapeDtypeStruct((m, d), _BF)],
        compiler_params=_params("arbitrary"),
        name="outproj",
    )(og, od, orw, w, x, g)


def _ffn_kernel(h_ref, wg_ref, wu_ref, wd_ref, x_ref, g_ref, *rest, nf, emit_x):
    if emit_x:
        x2_ref, hn_ref, acc_ref = rest
    else:
        hn_ref, acc_ref = rest
    f = pl.program_id(1)

    def tile():
        h = h_ref[...]
        gate = jnp.dot(h, wg_ref[...], preferred_element_type=_F32)
        up = jnp.dot(h, wu_ref[...], preferred_element_type=_F32)
        act = (_silu(gate) * up).astype(_BF)
        return jnp.dot(act, wd_ref[...], preferred_element_type=_F32)

    @pl.when(f == 0)
    def _():
        acc_ref[...] = tile()

    @pl.when((f > 0) & (f < nf - 1))
    def _():
        acc_ref[...] += tile()

    @pl.when(f == nf - 1)
    def _():
        x2 = x_ref[...] + (acc_ref[...] + tile())
        if emit_x:
            x2_ref[...] = x2
        hn_ref[...] = _rmsnorm_rows(x2, g_ref[...]).astype(hn_ref.dtype)


def _ffn_call(h, wg, wu, wd, l, x, g, tm, tf, last):
    m, d = x.shape
    nf = D_FF // tf
    assert nf >= 2 and nf * tf == D_FF
    row = pl.BlockSpec((tm, d), lambda i, f: (i, 0))
    if last:
        out_specs = [row]
        out_shape = [jax.ShapeDtypeStruct((m, d), _F32)]
    else:
        out_specs = [row, row]
        out_shape = [jax.ShapeDtypeStruct((m, d), _F32), jax.ShapeDtypeStruct((m, d), _BF)]
    return pl.pallas_call(
        functools.partial(_ffn_kernel, nf=nf, emit_x=not last),
        grid=(m // tm, nf),
        in_specs=[row, pl.BlockSpec((None, d, tf), lambda i, f: (l, 0, f)),
                  pl.BlockSpec((None, d, tf), lambda i, f: (l, 0, f)),
                  pl.BlockSpec((None, tf, d), lambda i, f: (l, f, 0)), row, pl.BlockSpec((1, d), lambda i, f: (0, 0))],
        out_specs=out_specs,
        out_shape=out_shape,
        scratch_shapes=[pltpu.VMEM((tm, d), _F32)],
        compiler_params=_params("arbitrary", "arbitrary"),
        name="ffn",
    )(h, wg, wu, wd, x, g)


def _gla_consts(c):
    nlev = int(np.log2(c))
    t = np.arange(c)
    blocks = [t[:, None] >= t[None, :],
              t[None, :] > t[:, None]]
    mq, mk, masks = [], [], []
    for lv in range(nlev):
        m = c >> lv
        half = m // 2
        blk, pos = t // m, t % m
        mid = blk * m + half
        upper = pos >= half
        mq.append(upper[:, None] & (t[None, :] >= mid[:, None]) & (t[None, :] <= t[:, None]))
        mk.append((~upper)[:, None] & (t[None, :] > t[:, None]) & (t[None, :] <= mid[:, None] - 1))
        masks.append((blk[:, None] == blk[None, :]) & upper[:, None] & (~upper)[None, :])
    mc = np.concatenate(blocks + mq + mk, axis=0).astype(np.float32)
    return jnp.asarray(mc, _BF), jnp.asarray(np.stack(masks).astype(np.float32)), nlev


def _gla_body(zg_ref, zs_ref, aup_ref, abias_ref, gn_ref, s0_ref, mc_ref, mk_ref,
              o_ref, sout_ref, s_scr, *, c, nlev, nc):
    step = pl.program_id(1)

    @pl.when(step == 0)
    def _():
        s_scr[...] = s0_ref[...]

    xa = _dot(zs_ref[...], aup_ref[...]) + abias_ref[...]
    la = (jnp.minimum(xa, 0.0) - jnp.log(1.0 + jnp.exp(-jnp.abs(xa)))) * (1.0 / GLA_GATE_TEMP)
    yield
    ex = jnp.exp(_cdot(mc_ref[...], la))
    yield
    q = zg_ref[:, 0:GLA_QK] * (GLA_DK ** -0.5)
    k = zg_ref[:, GLA_QK:2 * GLA_QK]
    qd = q * ex[0:c]
    kd = k * ex[c:2 * c]
    half = _lane_halves((c, LANES))
    eye = lax.broadcasted_iota(jnp.int32, (c, c), 0) == lax.broadcasted_iota(jnp.int32, (c, c), 1)
    ones = jnp.ones((c, LANES), _BF)
    heads = range(GLA_HEADS)
    sls = [slice(LANES * (h // 2), LANES * (h // 2 + 1)) for h in heads]
    sel = [(lambda x: jnp.where(half, x, 0.0)) if h % 2 == 0 else (lambda x: jnp.where(half, 0.0, x)) for h in heads]
    v = [zg_ref[:, 2 * GLA_QK + GLA_DV * h:2 * GLA_QK + GLA_DV * (h + 1)] for h in heads]
    s_old = [s_scr[p] for p in range(GLA_HEADS // 2)]
    prs = range(GLA_HEADS // 2)
    psl = [sls[2 * p] for p in prs]
    both = lambda x: jnp.concatenate([sel[0](x), sel[1](x)], axis=0)
    split = lambda xs: [x[e * c:(e + 1) * c] for x in xs for e in (0, 1)]
    a = [jnp.where(eye, x, 0.0) for x in split([_dot_nt(both(q[:, psl[p]]), k[:, psl[p]]) for p in prs])]
    yield
    for lv in range(nlev):
        exq = ex[(2 + lv) * c:(3 + lv) * c]
        kx = k * ex[(2 + nlev + lv) * c:(3 + nlev + lv) * c]
        qx = q * exq
        lvl = split([_dot_nt(both(qx[:, psl[p]]), kx[:, psl[p]]) for p in prs])
        a = [a[h] + mk_ref[lv] * lvl[h] for h in heads]
        yield
    inter = split([_dot(both(qd[:, psl[p]]), s_old[p]) for p in prs])
    o = [inter[h] + _dot(a[h], v[h]) for h in heads]
    yield
    ms = [jnp.mean(o[h] * o[h], axis=-1, keepdims=True) for h in heads]
    for p in range(GLA_HEADS // 2):
        sl = sls[2 * p]
        s_scr[p] = (jnp.exp(_cdot_tn(la[:, sl], ones)) * s_old[p]
                    + _dot_tn(both(kd[:, sl]), jnp.concatenate([v[2 * p], v[2 * p + 1]], axis=0)))
    yield
    for h in heads:
        g_h = zg_ref[:, 2 * GLA_QK + GLA_V + GLA_DV * h:2 * GLA_QK + GLA_V + GLA_DV * (h + 1)]
        y = o[h] * lax.rsqrt(ms[h] + NORM_EPS) * gn_ref[...] * _silu(g_h)
        o_ref[:, GLA_DV * h:GLA_DV * (h + 1)] = y.astype(o_ref.dtype)

    @pl.when(step == nc - 1)
    def _():
        sout_ref[...] = s_scr[...]


ROWS = 1
ROW_LAG = 6
MIXER_ORDER = ("rw", "gdn", "gla")


class _Part:
    def __init__(self, body, args, in_specs, out_specs, out_shape, scratch):
        self.body, self.args = body, list(args)
        self.in_specs, self.per_row = [s for s, _ in in_specs], [r for _, r in in_specs]
        self.out_specs, self.out_shape = [s for s, _ in out_specs], list(out_shape)
        self.scratch = [pltpu.VMEM((ROWS,) + shape, dtype) for shape, dtype in scratch]


def _full(a):
    return pl.BlockSpec(a.shape, lambda i, j: (0,) * a.ndim), False


def _per_batch(a):
    return pl.BlockSpec((ROWS,) + a.shape[1:], lambda i, j: (i,) + (0,) * (a.ndim - 1)), True


def _tok(c, w, col=0):
    return pl.BlockSpec((ROWS, c, w), lambda i, j: (i, j, col)), True


ZS_COL = ZG_W // ZS_W


def _gla_part(zg, zs, aup, abias, gn, s0, b, t, c, nc):
    mc, masks, nlev = _gla_consts(c)
    return _Part(
        functools.partial(_gla_body, c=c, nlev=nlev, nc=nc),
        [zg.reshape(b, t, -1), zs.reshape(b, t, -1), aup, abias, gn, s0, mc, masks],
        [_tok(c, ZG_W), _tok(c, ZS_W, ZS_COL), _full(aup), _full(abias), _full(gn), _per_batch(s0), _full(mc),
         _full(masks)],
        [_tok(c, GLA_V), _per_batch(s0)],
        [jax.ShapeDtypeStruct((b, t, GLA_V), _BF), jax.ShapeDtypeStruct(s0.shape, _F32)],
        [(s0.shape[1:], _F32)])


def _mixers_kernel(*refs, bodies, counts, per_row):
    n_in, n_out = sum(x[0] for x in counts), sum(x[1] for x in counts)
    rows = []
    for r in range(ROWS):
        i0, o0, s0 = 0, n_in, n_in + n_out
        gens = []
        for body, (ni, no, ns), flags in zip(bodies, counts, per_row):
            ins = [ref.at[r] if f else ref for ref, f in zip(refs[i0:i0 + ni], flags)]
            rest = [ref.at[r] for ref in refs[o0:o0 + no] + refs[s0:s0 + ns]]
            gens.append(body(*ins, *rest))
            i0, o0, s0 = i0 + ni, o0 + no, s0 + ns
        rows.append(gens)
    active, rounds = [], 0
    while active or rows:
        if rows and rounds % ROW_LAG == 0:
            active += rows.pop(0)
        active = [g for g in active if next(g, StopIteration) is not StopIteration]
        rounds += 1


def _mixers_call(parts, b, nc):
    outs = pl.pallas_call(
        functools.partial(_mixers_kernel, bodies=[p.body for p in parts],
                          counts=[(len(p.args), len(p.out_shape), len(p.scratch)) for p in parts],
                          per_row=[p.per_row for p in parts]),
        grid=(b // ROWS, nc),
        in_specs=[s for p in parts for s in p.in_specs],
        out_specs=[s for p in parts for s in p.out_specs],
        out_shape=[s for p in parts for s in p.out_shape],
        scratch_shapes=[s for p in parts for s in p.scratch],
        compiler_params=_params("arbitrary", "arbitrary"),
        name="mixers",
    )(*[a for p in parts for a in p.args])
    res, k = [], 0
    for p in parts:
        res.append(outs[k:k + len(p.out_shape)])
        k += len(p.out_shape)
    return res


def _gdn_body(zd_ref, zs_ref, cw_ref, alog_ref, dtb_ref, gn_ref, cache_ref, s0_ref, tril_ref, triu_ref, mlast_ref,
              o_ref, cout_ref, sout_ref, cbuf, s_scr, *, c, nc):
    step = pl.program_id(1)
    tail0 = 8 - (CONV_W - 1)

    @pl.when(step == 0)
    def _():
        cbuf[tail0:8, :] = cache_ref[...]
        s_scr[...] = s0_ref[...]

    x = zd_ref[:, 0:GDN_CONV_CH]
    cbuf[8:8 + c, :] = x
    y = cbuf[tail0:tail0 + c, :] * cw_ref[0:1, :]
    for j in range(1, CONV_W - 1):
        y = y + cbuf[tail0 + j:tail0 + j + c, :] * cw_ref[j:j + 1, :]
    y = y + x * cw_ref[CONV_W - 1:CONV_W, :]
    tail = cbuf[c + tail0:c + 8, :]
    cbuf[tail0:8, :] = tail

    @pl.when(step == nc - 1)
    def _():
        cout_ref[...] = tail

    yield
    ys = _silu(y)
    yield
    zs = zs_ref[...]
    lg = -jnp.exp(alog_ref[...]) * _softplus(zs + dtb_ref[...])
    bt = _sigmoid(zs)
    bcum = _cdot(tril_ref[...], lg)
    bcum_t = _cdot_tn(lg, triu_ref[...])
    bdl = _cdot(mlast_ref[...], lg)
    btot_t = _cdot_tn(lg, jnp.ones((c, LANES), _BF))
    lane = lax.broadcasted_iota(jnp.int32, (c, LANES), 1)
    rows = lax.broadcasted_iota(jnp.int32, (c, LANES), 0)
    causal = (rows >= lane) & (lane < c)
    strict = (rows > lane) & (lane < c)
    heads = range(GDN_HEADS)
    pick = lambda tile, j: jnp.sum(jnp.where(lane == j, tile, 0.0), axis=1, keepdims=True)
    bcol = [pick(bcum, ZS_DT + h) for h in heads]
    beta = [pick(bt, ZS_BETA + h) for h in heads]
    dlcol = [pick(bdl, ZS_DT + h) for h in heads]
    dec = [jnp.exp(jnp.where(causal, bcol[h] - bcum_t[ZS_DT + h:ZS_DT + h + 1, :], _NEG)) for h in heads]
    eb = [jnp.exp(bcol[h]) for h in heads]
    yield
    q, k, v = [], [], []
    for h in heads:
        q_h = ys[:, GDN_DK * h:GDN_DK * (h + 1)]
        k_h = ys[:, GDN_QK + GDN_DK * h:GDN_QK + GDN_DK * (h + 1)]
        q.append(q_h * lax.rsqrt(jnp.sum(q_h * q_h, axis=-1, keepdims=True) + L2_EPS) * (GDN_DK ** -0.5))
        k.append(k_h * lax.rsqrt(jnp.sum(k_h * k_h, axis=-1, keepdims=True) + L2_EPS))
        v.append(ys[:, 2 * GDN_QK + GDN_DV * h:2 * GDN_QK + GDN_DV * (h + 1)])
    qk = [jnp.concatenate([q[h], k[h]], axis=0) for h in heads]
    s_old = [s_scr[h] for h in heads]
    yield
    sc = [_dot_nt(qk[h], _rows_at(k[h], 0)) for h in heads]
    ps = [_dot(qk[h], s_old[h]) for h in heads]
    yield
    ws = _inverse_start([jnp.where(strict, beta[h] * dec[h] * sc[h][c:], 0.0) for h in heads], c)
    for _ in range(int(np.log2(c))):
        ws = _inverse_step(ws, c)
        yield
    delta = [_solve_with(ws[h], beta[h] * (v[h] - eb[h] * ps[h][c:])) for h in heads]
    yield
    o = [eb[h] * ps[h][:c] + _dot(sc[h][:c] * dec[h], _rows_at(delta[h], 0)) for h in heads]
    yield
    ms = [jnp.mean(o[h] * o[h], axis=-1, keepdims=True) for h in heads]
    for h in heads:
        elast = jnp.exp(btot_t[ZS_DT + h:ZS_DT + h + 1, :])
        s_scr[h] = elast * s_old[h] + _dot_tn(k[h] * jnp.exp(dlcol[h]), delta[h])
    yield
    for h in heads:
        g_h = zd_ref[:, GDN_CONV_CH + GDN_DV * h:GDN_CONV_CH + GDN_DV * (h + 1)]
        y_h = o[h] * lax.rsqrt(ms[h] + NORM_EPS) * gn_ref[...] * _silu(g_h)
        o_ref[:, GDN_DV * h:GDN_DV * (h + 1)] = y_h.astype(o_ref.dtype)

    @pl.when(step == nc - 1)
    def _():
        sout_ref[...] = s_scr[...]


def _tri_consts(c):
    t = np.arange(c)
    tril = (t[:, None] >= t[None, :]).astype(np.float32)
    mlast = (t[None, :] > t[:, None]).astype(np.float32)
    triu_wide = np.pad(tril.T, ((0, 0), (0, LANES - c)))
    return jnp.asarray(tril, _BF), jnp.asarray(triu_wide, _BF), jnp.asarray(mlast, _BF)


def _gdn_part(zd, zs, cw, alog, dtb, gn, cache, s0, b, t, c, nc):
    tril, triu, mlast = _tri_consts(c)
    return _Part(
        functools.partial(_gdn_body, c=c, nc=nc),
        [zd.reshape(b, t, -1), zs.reshape(b, t, -1), cw, alog, dtb, gn, cache, s0, tril, triu, mlast],
        [_tok(c, ZD_W), _tok(c, ZS_W, ZS_COL), _full(cw), _full(alog), _full(dtb), _full(gn), _per_batch(cache),
         _per_batch(s0), _full(tril), _full(triu), _full(mlast)],
        [_tok(c, GDN_V), _per_batch(cache), _per_batch(s0)],
        [jax.ShapeDtypeStruct((b, t, GDN_V), _BF), jax.ShapeDtypeStruct(cache.shape, _F32),
         jax.ShapeDtypeStruct(s0.shape, _F32)],
        [((c + 8, GDN_CONV_CH), _F32), (s0.shape[1:], _F32)])


def _rw_body(*refs, c, nc, first):
    if first:
        (zr_ref, mu_ref, w0_ref, wup_ref, a0_ref, aup_ref, gup_ref, kk_ref, ka_ref, rk_ref, lng_ref, lnb_ref,
         cache_ref, s0_ref, tril_ref, mlast_ref, o_ref, vf_out_ref, shout_ref, sout_ref, sbuf, s_scr) = refs
    else:
        (zr_ref, vf_ref, v0_ref, vdown_ref, vup_ref, mu_ref, w0_ref, wup_ref, a0_ref, aup_ref, gup_ref, kk_ref,
         ka_ref, rk_ref, lng_ref, lnb_ref, cache_ref, s0_ref, tril_ref, mlast_ref, o_ref, shout_ref, sout_ref,
         sbuf, s_scr) = refs
    step = pl.program_id(1)

    @pl.when(step == 0)
    def _():
        sbuf[7:8, :] = cache_ref[...]
        s_scr[...] = s0_ref[...]

    x = zr_ref[...]
    sbuf[8:8 + c, :] = x
    zprev = sbuf[7:7 + c, :]
    last = sbuf[c + 7:c + 8, :]
    sbuf[7:8, :] = last

    @pl.when(step == nc - 1)
    def _():
        shout_ref[...] = last

    yield
    zm = x + (zprev - x) * mu_ref[...]
    xr, xk, xv = zm[:, 0:RW_C], zm[:, RW_C:2 * RW_C], zm[:, 2 * RW_C:3 * RW_C]
    xwa = zm[:, 3 * RW_C:3 * RW_C + LANES]
    xg = zm[:, 3 * RW_C + LANES:]
    wlog = -_softplus(-(w0_ref[...] + _dot(jnp.tanh(xwa), wup_ref[...]))) - 0.5
    lw = -jnp.exp(wlog)
    a = _sigmoid(a0_ref[...] + _dot(xwa, aup_ref[...]))
    yield
    if first:
        vf_out_ref[...] = xv
    else:
        nu = _sigmoid(v0_ref[...] + _dot(_dot(xv, vdown_ref[...]), vup_ref[...]))
        xv = xv + (vf_ref[...] - xv) * nu
    kkp = xk * kk_ref[...]
    xk2 = xk * (1.0 + (a - 1.0) * ka_ref[...])
    gate = _dot(_sigmoid(xg), gup_ref[...])
    cum = _cdot(tril_ref[...], lw)
    yield
    e_c = jnp.exp(cum)
    e_cp = jnp.exp(cum - lw)
    e_nc = jnp.exp(-cum)
    e_dl = jnp.exp(_cdot(mlast_ref[...], lw))
    half = _lane_halves((c, LANES))
    half2 = _lane_halves((2 * c, LANES))
    rows = lax.broadcasted_iota(jnp.int32, (c, LANES), 0)
    cols = lax.broadcasted_iota(jnp.int32, (c, LANES), 1) & (HALF - 1)
    strict = (rows > cols) & (cols < c)
    causal = (rows >= cols) & (cols < c)
    srow = lax.broadcasted_iota(jnp.int32, (LANES, LANES), 0) < (LANES // 2)
    scol = lax.broadcasted_iota(jnp.int32, (LANES, LANES), 1) < (LANES // 2)
    same_head = srow == scol
    ones = jnp.ones((c, LANES), _BF)
    pairs = range(RW_HEADS // 2)
    sls = [slice(LANES * p, LANES * (p + 1)) for p in pairs]
    r, k, v = [xr[:, s] for s in sls], [xk2[:, s] for s in sls], [xv[:, s] for s in sls]
    kap = [kkp[:, s] for s in sls]
    kap = [x * lax.rsqrt(_half_sum(x * x, half) + L2_EPS) for x in kap]
    ahat = [-(kap[p] * a[:, sls[p]]) for p in pairs]
    x2 = [jnp.concatenate([kap[p] * e_cp[:, sls[p]], r[p] * e_c[:, sls[p]]], axis=0) for p in pairs]
    at = [ahat[p] * e_nc[:, sls[p]] for p in pairs]
    kt = [k[p] * e_nc[:, sls[p]] for p in pairs]
    s_bd = [s_scr[p] for p in pairs]
    bsum = [_half_sum(r[p] * k[p] * rk_ref[:, sls[p]], half) for p in pairs]
    yield
    ps = [_dot(x2[p], s_bd[p]) for p in pairs]
    sc = []
    for p in pairs:
        x2b = x2[p].astype(_BF)
        both = jnp.concatenate([jnp.where(half2, x2b, 0), jnp.where(half2, 0, x2b)], axis=0)
        both = _dot_nt(both, _rows_pair(at[p], kt[p]))
        sc += [both[:2 * c], both[2 * c:]]
    yield
    ws = _inverse_start([jnp.where(half & strict, -x[:c], 0.0) for x in sc], c)
    akv = [_dot(jnp.concatenate([jnp.where(half, 0.0, jnp.where(strict, sc[2 * p + e][:c], 0.0)) for e in (0, 1)], axis=0),
                _rows_at(v[p], HALF)) for p in pairs]
    rhs = [ps[p][:c] + jnp.where(half, akv[p][:c], akv[p][c:]) for p in pairs]
    for _ in range(int(np.log2(c))):
        ws = _inverse_step(ws, c)
        yield
    u = [_solve_with(jnp.concatenate([ws[2 * p], ws[2 * p + 1]], axis=0), rhs[p]) for p in pairs]
    u = [jnp.where(half, x[:c], x[c:]) for x in u]
    yield
    oh = [_dot(jnp.concatenate([jnp.where(causal, sc[2 * p + e][c:], 0.0) for e in (0, 1)], axis=0),
               _rows_pair(u[p], v[p])) for p in pairs]
    o = [ps[p][c:] + jnp.where(half, oh[p][:c], oh[p][c:]) for p in pairs]
    yield
    mean = [_half_sum(o[p], half) * (1.0 / RW_N) for p in pairs]
    yield
    for p in pairs:
        dl = e_dl[:, sls[p]]
        upd = _dot_tn(jnp.concatenate([ahat[p] * dl, k[p] * dl], axis=0), jnp.concatenate([u[p], v[p]], axis=0))
        s_scr[p] = jnp.exp(_cdot_tn(lw[:, sls[p]], ones)) * s_bd[p] + jnp.where(same_head, upd, 0.0)
    dev = [o[p] - mean[p] for p in pairs]
    var = [_half_sum(dev[p] * dev[p], half) * (1.0 / RW_N) for p in pairs]
    yield
    for p in pairs:
        on = dev[p] * lax.rsqrt(var[p] + RW_GN_EPS) * lng_ref[:, sls[p]] + lnb_ref[:, sls[p]]
        o_ref[:, sls[p]] = ((on + bsum[p] * v[p]) * gate[:, sls[p]]).astype(o_ref.dtype)

    @pl.when(step == nc - 1)
    def _():
        sout_ref[...] = s_scr[...]


def _rw_part(zr, vf, vmix, prm, cache, s0, b, t, c, nc):
    tril, _, mlast = _tri_consts(c)
    first = vf is None
    args = ([zr.reshape(b, t, -1)] + ([] if first else [vf.reshape(b, t, -1)] + list(vmix)) + list(prm)
            + [cache, s0, tril, mlast])
    in_specs = ([_tok(c, RW_PROJ)] + ([] if first else [_tok(c, RW_C)] + [_full(a) for a in vmix])
                + [_full(a) for a in prm] + [_per_batch(cache), _per_batch(s0), _full(tril), _full(mlast)])
    out_specs = [_tok(c, RW_C)] + ([_tok(c, RW_C)] if first else []) + [_per_batch(cache), _per_batch(s0)]
    out_shape = ([jax.ShapeDtypeStruct((b, t, RW_C), _BF)] + ([jax.ShapeDtypeStruct((b, t, RW_C), _F32)] if first else [])
                 + [jax.ShapeDtypeStruct(cache.shape, _F32), jax.ShapeDtypeStruct(s0.shape, _F32)])
    return _Part(functools.partial(_rw_body, c=c, nc=nc, first=first), args, in_specs, out_specs, out_shape,
                 [((c + 8, RW_PROJ), _F32), (s0.shape[1:], _F32)])


def _prep_weights(p):
    bf = lambda a: a.astype(_BF)
    w_in = bf(p["w_in"])
    o1 = GLA_PROJ
    o2 = GLA_PROJ + GDN_PROJ
    lora0 = 2 * GLA_QK + GLA_V
    narrow = jnp.concatenate([w_in[:, :, lora0:lora0 + GLA_LORA],
                              w_in[:, :, o1 + GDN_CONV_CH:o1 + GDN_CONV_CH + 2 * GDN_HEADS]], axis=2)
    w_small = jnp.pad(narrow, ((0, 0), (0, 0), (0, ZS_W - narrow.shape[2])))
    w_gla = jnp.concatenate([w_in[:, :, 0:lora0], w_in[:, :, lora0 + GLA_LORA:o1], w_small], axis=2)
    w_gdn = jnp.concatenate([w_in[:, :, o1:o1 + GDN_CONV_CH], w_in[:, :, o1 + GDN_CONV_CH + 2 * GDN_HEADS:o2]], axis=2)
    w_rw = w_in[:, :, o2:]
    pad_rows = lambda a, lo, hi: jnp.pad(a, ((0, 0), (lo, hi), (0, 0)))
    pad_lanes = lambda a, lo: jnp.pad(a, ((0, 0), (lo, ZS_W - lo - a.shape[1])))[:, None, :]
    row = lambda a: a.reshape(a.shape[0], 1, -1)
    return dict(
        w_gla=w_gla, w_gdn=w_gdn, w_rw=w_rw,
        w_out=bf(p["w_out"]), w_g=bf(p["w_ffn_gate"]), w_u=bf(p["w_ffn_up"]), w_d=bf(p["w_ffn_down"]),
        gla_aup=bf(pad_rows(p["gla_a_up"], 0, ZS_W - GLA_LORA)), gla_abias=row(p["gla_a_bias"]),
        gla_gn=row(p["gla_norm_g"]),
        gdn_cw=p["gdn_conv_w"], gdn_alog=pad_lanes(p["gdn_A_log"], ZS_DT), gdn_dtb=pad_lanes(p["gdn_dt_bias"], ZS_DT),
        gdn_gn=row(p["gdn_norm_g"]),
        rw_mu=row(p["rw_mu"]), rw_w0=row(p["rw_w0"]), rw_wup=bf(pad_rows(p["rw_w_up"], 0, LANES - RW_DECAY_LORA)),
        rw_a0=row(p["rw_a0"]), rw_aup=bf(pad_rows(p["rw_a_up"], RW_DECAY_LORA, 0)), rw_gup=bf(p["rw_g_up"]),
        rw_kk=row(p["rw_k_k"]), rw_ka=row(p["rw_k_a"]), rw_rk=p["rw_r_k"].reshape(DEPTH, 1, RW_C),
        rw_lng=row(p["rw_ln_g"]), rw_lnb=row(p["rw_ln_b"]),
        rw_v0=row(p["rw_v0"]), rw_vdown=bf(p["rw_v_down"]), rw_vup=bf(p["rw_v_up"]),
        norm1=row(p["norm1_g"]), norm2=row(p["norm2_g"]), final=p["final_norm_g"].reshape(1, -1),
    )


def _pair_block_diag(s):
    b, h, n, _ = s.shape
    s = s.reshape(b, h // 2, 2, n, n)
    z = jnp.zeros_like(s[:, :, 0])
    top = jnp.concatenate([s[:, :, 0], z], axis=-1)
    bot = jnp.concatenate([z, s[:, :, 1]], axis=-1)
    return jnp.concatenate([top, bot], axis=-2)


def _pair_blocks(s):
    b, hp, n2, _ = s.shape
    n = n2 // 2
    return jnp.stack([s[:, :, :n, :n], s[:, :, n:, n:]], axis=2).reshape(b, 2 * hp, n, n)


def _trunk(x, s_gla, s_gdn, c_gdn, s_rw, c_rw, w):
    b, t, d = x.shape
    m = b * t
    tm = min(512, m)
    tp = min(1024, m)
    c = min(CHUNK, t)
    nc = t // c
    xf = x.reshape(m, d)
    h = _norm_call(xf, w["norm1"][0], tm)
    outs = ([], [], [], [], [])
    vf = None
    for l in range(DEPTH):
        zg = _mm_call(h, w["w_gla"], l, tp, ZG_W + ZS_W, "proj_gla")
        zd = _mm_call(h, w["w_gdn"], l, tp, ZD_W // 2, "proj_gdn")
        zr = _mm_call(h, w["w_rw"], l, tp, RW_PROJ // 2, "proj_rw")
        zs = zg
        prm = [w[n][l] for n in ("rw_mu", "rw_w0", "rw_wup", "rw_a0", "rw_aup", "rw_gup", "rw_kk", "rw_ka",
                                 "rw_rk", "rw_lng", "rw_lnb")]
        vmix = None if l == 0 else [w[n][l - 1] for n in ("rw_v0", "rw_vdown", "rw_vup")]
        parts = dict(
            rw=_rw_part(zr, vf, vmix, prm, c_rw[l], _pair_block_diag(s_rw[l]), b, t, c, nc),
            gdn=_gdn_part(zd, zs, w["gdn_cw"][l], w["gdn_alog"][l], w["gdn_dtb"][l], w["gdn_gn"][l],
                          c_gdn[l], s_gdn[l], b, t, c, nc),
            gla=_gla_part(zg, zs, w["gla_aup"][l], w["gla_abias"][l], w["gla_gn"][l],
                          s_gla[l].reshape(b, GLA_HEADS // 2, 2 * GLA_DK, GLA_DV), b, t, c, nc))
        outs_by = dict(zip(MIXER_ORDER, _mixers_call([parts[n] for n in MIXER_ORDER], b, nc)))
        res, (od, cg, sd), (og, sg) = outs_by["rw"], outs_by["gdn"], outs_by["gla"]
        if l == 0:
            orw, vf, cr, sr = res
        else:
            orw, cr, sr = res
        flat = lambda a: a.reshape(m, a.shape[-1])
        xf, h2 = _outproj_call(flat(og), flat(od), flat(orw), w["w_out"], l, xf, w["norm2"][l], tm)
        last = l == DEPTH - 1
        g_next = w["final"] if last else w["norm1"][l + 1]
        res = _ffn_call(h2, w["w_g"], w["w_u"], w["w_d"], l, xf, g_next, tm, 512, last)
        if last:
            (y,) = res
        else:
            xf, h = res
        for lst, arr in zip(outs, (sg.reshape(b, GLA_HEADS, GLA_DK, GLA_DV), sd, cg, _pair_blocks(sr), cr)):
            lst.append(arr)
    return y.reshape(b, t, d), [jnp.stack(lst) for lst in outs]


def kernel(x_prompt, x_sample, state_gla, state_gdn, cache_gdn_conv, state_rwkv, cache_rwkv_shift, norm1_g, w_in, gla_a_up, gla_a_bias, gla_norm_g, gdn_conv_w, gdn_A_log, gdn_dt_bias, gdn_norm_g, rw_mu, rw_w0, rw_w_up, rw_a0, rw_a_up, rw_v0, rw_v_down, rw_v_up, rw_g_up, rw_k_k, rw_k_a, rw_r_k, rw_ln_g, rw_ln_b, w_out, norm2_g, w_ffn_gate, w_ffn_up, w_ffn_down, final_norm_g):
    p = dict(norm1_g=norm1_g, w_in=w_in, gla_a_up=gla_a_up, gla_a_bias=gla_a_bias, gla_norm_g=gla_norm_g,
             gdn_conv_w=gdn_conv_w, gdn_A_log=gdn_A_log, gdn_dt_bias=gdn_dt_bias, gdn_norm_g=gdn_norm_g,
             rw_mu=rw_mu, rw_w0=rw_w0, rw_w_up=rw_w_up, rw_a0=rw_a0, rw_a_up=rw_a_up, rw_v0=rw_v0,
             rw_v_down=rw_v_down, rw_v_up=rw_v_up, rw_g_up=rw_g_up, rw_k_k=rw_k_k, rw_k_a=rw_k_a,
             rw_r_k=rw_r_k, rw_ln_g=rw_ln_g, rw_ln_b=rw_ln_b, w_out=w_out, norm2_g=norm2_g,
             w_ffn_gate=w_ffn_gate, w_ffn_up=w_ffn_up, w_ffn_down=w_ffn_down, final_norm_g=final_norm_g)
    w = _prep_weights(p)
    bp = x_prompt.shape[0]
    zeros = lambda a: jnp.zeros((DEPTH, bp) + a.shape[2:], a.dtype)
    y_p, st_p = _trunk(x_prompt, zeros(state_gla), zeros(state_gdn), zeros(cache_gdn_conv), zeros(state_rwkv),
                       zeros(cache_rwkv_shift), w)
    y_s, st_s = _trunk(x_sample, state_gla, state_gdn, cache_gdn_conv, state_rwkv, cache_rwkv_shift, w)
    return (y_p, y_s, *st_p, *st_s)
```

```python
import functools

import numpy as np
import jax
import jax.numpy as jnp
from jax import lax
from jax.experimental import pallas as pl
from jax.experimental.pallas import tpu as pltpu

D_MODEL = 2048
DEPTH = 4
CHUNK = 64
NORM_EPS = 1e-6
L2_EPS = 1e-6
GLA_HEADS, GLA_DK, GLA_DV, GLA_LORA, GLA_GATE_TEMP = 4, 64, 128, 16, 16.0
GDN_HEADS, GDN_DK, GDN_DV, CONV_W = 6, 128, 128, 4
RW_HEADS, RW_N = 12, 64
RW_DECAY_LORA, RW_AAA_LORA, RW_MV_LORA, RW_GATE_LORA = 64, 64, 32, 128
RW_GN_EPS = 64e-5

GLA_QK = GLA_HEADS * GLA_DK
GLA_V = GLA_HEADS * GLA_DV
GDN_QK = GDN_HEADS * GDN_DK
GDN_V = GDN_HEADS * GDN_DV
GDN_CONV_CH = 2 * GDN_QK + GDN_V
RW_C = RW_HEADS * RW_N
GLA_PROJ = 2 * GLA_QK + GLA_V + GLA_LORA + GLA_V
GDN_PROJ = GDN_CONV_CH + 2 * GDN_HEADS + GDN_V
RW_PROJ = 3 * RW_C + RW_DECAY_LORA + RW_AAA_LORA + RW_GATE_LORA
D_FF = -(-8 * D_MODEL // (3 * 256)) * 256

LANES = 128
SUBLANES = 8
ZG_W = 2 * GLA_QK + 2 * GLA_V
ZD_W = GDN_CONV_CH + GDN_V
ZS_W = LANES
ZS_BETA = GLA_LORA
ZS_DT = GLA_LORA + GDN_HEADS
VMEM_LIMIT = 52 * 1024 * 1024
ROW_TILE = 512
PROJ_ROW_TILE = 1024
FFN_TF = 512

_BF = jnp.bfloat16
_F32 = jnp.float32
_NEG = -1e30


def _dot(a, b):
    return jnp.dot(a.astype(_BF), b.astype(_BF), preferred_element_type=_F32)


def _dot_nt(a, b):
    return lax.dot_general(a.astype(_BF), b.astype(_BF), (((1,), (1,)), ((), ())),
                           preferred_element_type=_F32)


def _dot_tn(a, b):
    return lax.dot_general(a.astype(_BF), b.astype(_BF), (((0,), (0,)), ((), ())),
                           preferred_element_type=_F32)


def _split2(x):
    hi = x.astype(_BF)
    lo = (x - hi.astype(_F32)).astype(_BF)
    return hi, lo


def _cdot(c, x):
    hi, lo = _split2(x)
    n = x.shape[1]
    if n > LANES:
        return jnp.dot(c, hi, preferred_element_type=_F32) + jnp.dot(c, lo, preferred_element_type=_F32)
    y = jnp.dot(c, jnp.concatenate([hi, lo], axis=1), preferred_element_type=_F32)
    return y[:, :n] + y[:, n:]


def _cdot_tn(x, c):
    n = x.shape[1]
    y = lax.dot_general(jnp.concatenate(_split2(x), axis=1), c, (((0,), (0,)), ((), ())),
                        preferred_element_type=_F32)
    return y[:n] + y[n:]


def _sigmoid(x):
    return 1.0 / (1.0 + jnp.exp(-x))


def _silu(x):
    return x * _sigmoid(x)


def _softplus(x):
    return jnp.maximum(x, 0.0) + jnp.log(1.0 + jnp.exp(-jnp.abs(x)))


HALF = LANES // 2


def _rows_at(x, first, total=LANES):
    n, w = x.shape
    parts = [jnp.zeros((first, w), x.dtype)] if first else []
    parts.append(x)
    if total - first - n:
        parts.append(jnp.zeros((total - first - n, w), x.dtype))
    return jnp.concatenate(parts, axis=0) if len(parts) > 1 else x


def _rows_pair(a, b):
    return jnp.concatenate([_rows_at(a, 0, HALF), _rows_at(b, 0, HALF)], axis=0)


def _inverse_start(lows, n):
    rows = lax.broadcasted_iota(jnp.int32, (n, LANES), 0)
    lane = lax.broadcasted_iota(jnp.int32, (n, LANES), 1)
    return [jnp.where(lane == rows + HALF, 1.0, 0.0) - low for low in lows]


def _inverse_step(ws, n):
    keep = lax.broadcasted_iota(jnp.int32, (n, LANES), 1) >= HALF
    wbs = [w.astype(_BF) for w in ws]
    ys = [jnp.dot(wb, _rows_at(wb, 0), preferred_element_type=_F32) for wb in wbs]
    return [y + jnp.where(keep, w, 0.0) for y, w in zip(ys, ws)]


def _solve_with(w, rhs):
    n = rhs.shape[1]
    both = jnp.concatenate([_rows_at(part, HALF) for part in _split2(rhs)], axis=1)
    y = jnp.dot(w.astype(_BF), both, preferred_element_type=_F32)
    return y[:, :n] + y[:, n:]


def _lane_halves(shape):
    lane = lax.broadcasted_iota(jnp.int32, shape, len(shape) - 1)
    return lane < (LANES // 2)


def _half_sum(x, half):
    s0 = jnp.sum(jnp.where(half, x, 0.0), axis=-1, keepdims=True)
    s1 = jnp.sum(jnp.where(half, 0.0, x), axis=-1, keepdims=True)
    return jnp.where(half, s0, s1)


def _rmsnorm_rows(x, g):
    return x * lax.rsqrt(jnp.mean(x * x, axis=-1, keepdims=True) + NORM_EPS) * g


def _params(*sem):
    return pltpu.CompilerParams(dimension_semantics=sem, vmem_limit_bytes=VMEM_LIMIT)


def _norm_kernel(x_ref, g_ref, o_ref):
    o_ref[...] = _rmsnorm_rows(x_ref[...], g_ref[...]).astype(o_ref.dtype)


def _norm_call(x, g, tm):
    m, d = x.shape
    return pl.pallas_call(
        _norm_kernel,
        grid=(m // tm,),
        in_specs=[pl.BlockSpec((tm, d), lambda i: (i, 0)), pl.BlockSpec((1, d), lambda i: (0, 0))],
        out_specs=pl.BlockSpec((tm, d), lambda i: (i, 0)),
        out_shape=jax.ShapeDtypeStruct((m, d), _BF),
        compiler_params=_params("arbitrary"),
        name="rmsnorm",
    )(x, g)


def _mm_kernel(a_ref, w_ref, o_ref):
    o_ref[...] = jnp.dot(a_ref[...], w_ref[...], preferred_element_type=_F32)


def _mm_call(a, w, l, tm, tn, name):
    m, k = a.shape
    n = w.shape[2]
    return pl.pallas_call(
        _mm_kernel,
        grid=(n // tn, m // tm),
        in_specs=[pl.BlockSpec((tm, k), lambda j, i: (i, 0)), pl.BlockSpec((None, k, tn), lambda j, i: (l, 0, j))],
        out_specs=pl.BlockSpec((tm, tn), lambda j, i: (i, j)),
        out_shape=jax.ShapeDtypeStruct((m, n), _F32),
        compiler_params=_params("arbitrary", "arbitrary"),
        name=name,
    )(a, w)


def _outproj_kernel(og_ref, od_ref, or_ref, w_ref, x_ref, g_ref, x1_ref, h_ref):
    d = functools.partial(jnp.dot, preferred_element_type=_F32)
    acc = (d(og_ref[...], w_ref[0:GLA_V, :]) + d(od_ref[...], w_ref[GLA_V:GLA_V + GDN_V, :])
           + d(or_ref[...], w_ref[GLA_V + GDN_V:, :]))
    x1 = x_ref[...] + acc
    x1_ref[...] = x1
    h_ref[...] = _rmsnorm_rows(x1, g_ref[...]).astype(h_ref.dtype)


def _outproj_call(og, od, orw, w, l, x, g, tm):
    m, d = x.shape
    row = lambda width: pl.BlockSpec((tm, width), lambda i: (i, 0))
    return pl.pallas_call(
        _outproj_kernel,
        grid=(m // tm,),
        in_specs=[row(GLA_V), row(GDN_V), row(RW_C), pl.BlockSpec((None,) + w.shape[1:], lambda i: (l, 0, 0)), row(d),
                  pl.BlockSpec((1, d), lambda i: (0, 0))],
        out_specs=[row(d), row(d)],
        out_shape=[jax.ShapeDtypeStruct((m, d), _F32), jax.ShapeDtypeStruct((m, d), _BF)],
        compiler_params=_params("arbitrary"),
        name="outproj",
    )(og, od, orw, w, x, g)


def _ffn_kernel(h_ref, wg_ref, wu_ref, wd_ref, x_ref, g_ref, *rest, nf, emit_x):
    if emit_x:
        x2_ref, hn_ref, acc_ref = rest
    else:
        hn_ref, acc_ref = rest
    f = pl.program_id(1)

    def tile():
        h = h_ref[...]
        gate = jnp.dot(h, wg_ref[...], preferred_element_type=_F32)
        up = jnp.dot(h, wu_ref[...], preferred_element_type=_F32)
        act = (_silu(gate) * up).astype(_BF)
        return jnp.dot(act, wd_ref[...], preferred_element_type=_F32)

    @pl.when(f == 0)
    def _():
        acc_ref[...] = tile()

    @pl.when((f > 0) & (f < nf - 1))
    def _():
        acc_ref[...] += tile()

    @pl.when(f == nf - 1)
    def _():
        x2 = x_ref[...] + (acc_ref[...] + tile())
        if emit_x:
            x2_ref[...] = x2
        hn_ref[...] = _rmsnorm_rows(x2, g_ref[...]).astype(hn_ref.dtype)


def _ffn_call(h, wg, wu, wd, l, x, g, tm, tf, last):
    m, d = x.shape
    nf = D_FF // tf
    assert nf >= 2 and nf * tf == D_FF
    row = pl.BlockSpec((tm, d), lambda i, f: (i, 0))
    if last:
        out_specs = [row]
        out_shape = [jax.ShapeDtypeStruct((m, d), _F32)]
    else:
        out_specs = [row, row]
        out_shape = [jax.ShapeDtypeStruct((m, d), _F32), jax.ShapeDtypeStruct((m, d), _BF)]
    return pl.pallas_call(
        functools.partial(_ffn_kernel, nf=nf, emit_x=not last),
        grid=(m // tm, nf),
        in_specs=[row, pl.BlockSpec((None, d, tf), lambda i, f: (l, 0, f)),
                  pl.BlockSpec((None, d, tf), lambda i, f: (l, 0, f)),
                  pl.BlockSpec((None, tf, d), lambda i, f: (l, f, 0)), row, pl.BlockSpec((1, d), lambda i, f: (0, 0))],
        out_specs=out_specs,
        out_shape=out_shape,
        scratch_shapes=[pltpu.VMEM((tm, d), _F32)],
        compiler_params=_params("arbitrary", "arbitrary"),
        name="ffn",
    )(h, wg, wu, wd, x, g)


def _gla_consts(c):
    nlev = int(np.log2(c))
    t = np.arange(c)
    blocks = [t[:, None] >= t[None, :],
              t[None, :] > t[:, None]]
    mq, mk, masks = [], [], []
    for lv in range(nlev):
        m = c >> lv
        half = m // 2
        blk, pos = t // m, t % m
        mid = blk * m + half
        upper = pos >= half
        mq.append(upper[:, None] & (t[None, :] >= mid[:, None]) & (t[None, :] <= t[:, None]))
        mk.append((~upper)[:, None] & (t[None, :] > t[:, None]) & (t[None, :] <= mid[:, None] - 1))
        masks.append((blk[:, None] == blk[None, :]) & upper[:, None] & (~upper)[None, :])
    mc = np.concatenate(blocks + mq + mk, axis=0).astype(np.float32)
    return jnp.asarray(mc, _BF), jnp.asarray(np.stack(masks).astype(np.float32)), nlev


def _gla_body(zg_ref, zs_ref, aup_ref, abias_ref, gn_ref, s0_ref, mc_ref, mk_ref,
              o_ref, sout_ref, s_scr, *, c, nlev, nc):
    step = pl.program_id(1)

    @pl.when(step == 0)
    def _():
        s_scr[...] = s0_ref[...]

    xa = _dot(zs_ref[...], aup_ref[...]) + abias_ref[...]
    la = (jnp.minimum(xa, 0.0) - jnp.log(1.0 + jnp.exp(-jnp.abs(xa)))) * (1.0 / GLA_GATE_TEMP)
    yield
    ex = jnp.exp(_cdot(mc_ref[...], la))
    yield
    q = zg_ref[:, 0:GLA_QK] * (GLA_DK ** -0.5)
    k = zg_ref[:, GLA_QK:2 * GLA_QK]
    qd = q * ex[0:c]
    kd = k * ex[c:2 * c]
    half = _lane_halves((c, LANES))
    eye = lax.broadcasted_iota(jnp.int32, (c, c), 0) == lax.broadcasted_iota(jnp.int32, (c, c), 1)
    ones = jnp.ones((c, LANES), _BF)
    heads = range(GLA_HEADS)
    sls = [slice(LANES * (h // 2), LANES * (h // 2 + 1)) for h in heads]
    sel = [(lambda x: jnp.where(half, x, 0.0)) if h % 2 == 0 else (lambda x: jnp.where(half, 0.0, x)) for h in heads]
    v = [zg_ref[:, 2 * GLA_QK + GLA_DV * h:2 * GLA_QK + GLA_DV * (h + 1)] for h in heads]
    s_old = [s_scr[p] for p in range(GLA_HEADS // 2)]
    prs = range(GLA_HEADS // 2)
    psl = [sls[2 * p] for p in prs]
    both = lambda x: jnp.concatenate([sel[0](x), sel[1](x)], axis=0)
    split = lambda xs: [x[e * c:(e + 1) * c] for x in xs for e in (0, 1)]
    a = [jnp.where(eye, x, 0.0) for x in split([_dot_nt(both(q[:, psl[p]]), k[:, psl[p]]) for p in prs])]
    yield
    for lv in range(nlev):
        exq = ex[(2 + lv) * c:(3 + lv) * c]
        kx = k * ex[(2 + nlev + lv) * c:(3 + nlev + lv) * c]
        qx = q * exq
        lvl = split([_dot_nt(both(qx[:, psl[p]]), kx[:, psl[p]]) for p in prs])
        a = [a[h] + mk_ref[lv] * lvl[h] for h in heads]
        yield
    inter = split([_dot(both(qd[:, psl[p]]), s_old[p]) for p in prs])
    o = [inter[h] + _dot(a[h], v[h]) for h in heads]
    yield
    ms = [jnp.mean(o[h] * o[h], axis=-1, keepdims=True) for h in heads]
    for p in range(GLA_HEADS // 2):
        sl = sls[2 * p]
        s_scr[p] = (jnp.exp(_cdot_tn(la[:, sl], ones)) * s_old[p]
                    + _dot_tn(both(kd[:, sl]), jnp.concatenate([v[2 * p], v[2 * p + 1]], axis=0)))
    yield
    for h in heads:
        g_h = zg_ref[:, 2 * GLA_QK + GLA_V + GLA_DV * h:2 * GLA_QK + GLA_V + GLA_DV * (h + 1)]
        y = o[h] * lax.rsqrt(ms[h] + NORM_EPS) * gn_ref[...] * _silu(g_h)
        o_ref[:, GLA_DV * h:GLA_DV * (h + 1)] = y.astype(o_ref.dtype)

    @pl.when(step == nc - 1)
    def _():
        sout_ref[...] = s_scr[...]


MIXER_ORDER = ("rw", "gdn", "gla")


class _Part:
    def __init__(self, body, args, in_specs, out_specs, out_shape, scratch):
        self.body, self.args, self.in_specs = body, list(args), list(in_specs)
        self.out_specs, self.out_shape = list(out_specs), list(out_shape)
        self.scratch = [pltpu.VMEM(shape, dtype) for shape, dtype in scratch]


def _full(a):
    return pl.BlockSpec(a.shape, lambda i, j: (0,) * a.ndim)


def _per_batch(a):
    return pl.BlockSpec((None,) + a.shape[1:], lambda i, j: (i,) + (0,) * (a.ndim - 1))


def _tok(c, w, col=0):
    return pl.BlockSpec((None, c, w), lambda i, j: (i, j, col))


ZS_COL = ZG_W // ZS_W


def _gla_part(zg, zs, aup, abias, gn, s0, b, t, c, nc):
    mc, masks, nlev = _gla_consts(c)
    return _Part(
        functools.partial(_gla_body, c=c, nlev=nlev, nc=nc),
        [zg.reshape(b, t, -1), zs.reshape(b, t, -1), aup, abias, gn, s0, mc, masks],
        [_tok(c, ZG_W), _tok(c, ZS_W, ZS_COL), _full(aup), _full(abias), _full(gn), _per_batch(s0), _full(mc),
         _full(masks)],
        [_tok(c, GLA_V), _per_batch(s0)],
        [jax.ShapeDtypeStruct((b, t, GLA_V), _BF), jax.ShapeDtypeStruct(s0.shape, _F32)],
        [(s0.shape[1:], _F32)])


def _mixers_kernel(*refs, bodies, counts):
    n_in, n_out = sum(x[0] for x in counts), sum(x[1] for x in counts)
    i0, o0, s0 = 0, n_in, n_in + n_out
    gens = []
    for body, (ni, no, ns) in zip(bodies, counts):
        gens.append(body(*refs[i0:i0 + ni], *refs[o0:o0 + no], *refs[s0:s0 + ns]))
        i0, o0, s0 = i0 + ni, o0 + no, s0 + ns
    while gens:
        gens = [g for g in gens if next(g, StopIteration) is not StopIteration]


def _mixers_call(parts, b, nc):
    outs = pl.pallas_call(
        functools.partial(_mixers_kernel, bodies=[p.body for p in parts],
                          counts=[(len(p.args), len(p.out_shape), len(p.scratch)) for p in parts]),
        grid=(b, nc),
        in_specs=[s for p in parts for s in p.in_specs],
        out_specs=[s for p in parts for s in p.out_specs],
        out_shape=[s for p in parts for s in p.out_shape],
        scratch_shapes=[s for p in parts for s in p.scratch],
        compiler_params=_params("arbitrary", "arbitrary"),
        name="mixers",
    )(*[a for p in parts for a in p.args])
    res, k = [], 0
    for p in parts:
        res.append(outs[k:k + len(p.out_shape)])
        k += len(p.out_shape)
    return res


def _gdn_body(zd_ref, zs_ref, cw_ref, alog_ref, dtb_ref, gn_ref, cache_ref, s0_ref, tril_ref, triu_ref, mlast_ref,
              o_ref, cout_ref, sout_ref, cbuf, s_scr, *, c, nc):
    step = pl.program_id(1)
    top = SUBLANES
    tail0 = top - (CONV_W - 1)

    @pl.when(step == 0)
    def _():
        cbuf[tail0:top, :] = cache_ref[...]
        s_scr[...] = s0_ref[...]

    x = zd_ref[:, 0:GDN_CONV_CH]
    cbuf[top:top + c, :] = x
    y = cbuf[tail0:tail0 + c, :] * cw_ref[0:1, :]
    for j in range(1, CONV_W - 1):
        y = y + cbuf[tail0 + j:tail0 + j + c, :] * cw_ref[j:j + 1, :]
    y = y + x * cw_ref[CONV_W - 1:CONV_W, :]
    tail = cbuf[c + tail0:c + top, :]
    cbuf[tail0:top, :] = tail

    @pl.when(step == nc - 1)
    def _():
        cout_ref[...] = tail

    yield
    ys = _silu(y)
    yield
    zs = zs_ref[...]
    lg = -jnp.exp(alog_ref[...]) * _softplus(zs + dtb_ref[...])
    bt = _sigmoid(zs)
    bcum = _cdot(tril_ref[...], lg)
    bcum_t = _cdot_tn(lg, triu_ref[...])
    bdl = _cdot(mlast_ref[...], lg)
    btot_t = _cdot_tn(lg, jnp.ones((c, LANES), _BF))
    lane = lax.broadcasted_iota(jnp.int32, (c, LANES), 1)
    rows = lax.broadcasted_iota(jnp.int32, (c, LANES), 0)
    causal = (rows >= lane) & (lane < c)
    strict = (rows > lane) & (lane < c)
    heads = range(GDN_HEADS)
    pick = lambda tile, j: jnp.sum(jnp.where(lane == j, tile, 0.0), axis=1, keepdims=True)
    bcol = [pick(bcum, ZS_DT + h) for h in heads]
    beta = [pick(bt, ZS_BETA + h) for h in heads]
    dlcol = [pick(bdl, ZS_DT + h) for h in heads]
    dec = [jnp.exp(jnp.where(causal, bcol[h] - bcum_t[ZS_DT + h:ZS_DT + h + 1, :], _NEG)) for h in heads]
    eb = [jnp.exp(bcol[h]) for h in heads]
    yield
    q, k, v = [], [], []
    for h in heads:
        q_h = ys[:, GDN_DK * h:GDN_DK * (h + 1)]
        k_h = ys[:, GDN_QK + GDN_DK * h:GDN_QK + GDN_DK * (h + 1)]
        q.append(q_h * lax.rsqrt(jnp.sum(q_h * q_h, axis=-1, keepdims=True) + L2_EPS) * (GDN_DK ** -0.5))
        k.append(k_h * lax.rsqrt(jnp.sum(k_h * k_h, axis=-1, keepdims=True) + L2_EPS))
        v.append(ys[:, 2 * GDN_QK + GDN_DV * h:2 * GDN_QK + GDN_DV * (h + 1)])
    qk = [jnp.concatenate([q[h], k[h]], axis=0) for h in heads]
    s_old = [s_scr[h] for h in heads]
    yield
    sc = [_dot_nt(qk[h], _rows_at(k[h], 0)) for h in heads]
    ps = [_dot(qk[h], s_old[h]) for h in heads]
    yield
    ws = _inverse_start([jnp.where(strict, beta[h] * dec[h] * sc[h][c:], 0.0) for h in heads], c)
    for _ in range(int(np.log2(c))):
        ws = _inverse_step(ws, c)
        yield
    delta = [_solve_with(ws[h], beta[h] * (v[h] - eb[h] * ps[h][c:])) for h in heads]
    yield
    o = [eb[h] * ps[h][:c] + _dot(sc[h][:c] * dec[h], _rows_at(delta[h], 0)) for h in heads]
    yield
    ms = [jnp.mean(o[h] * o[h], axis=-1, keepdims=True) for h in heads]
    for h in heads:
        elast = jnp.exp(btot_t[ZS_DT + h:ZS_DT + h + 1, :])
        s_scr[h] = elast * s_old[h] + _dot_tn(k[h] * jnp.exp(dlcol[h]), delta[h])
    yield
    for h in heads:
        g_h = zd_ref[:, GDN_CONV_CH + GDN_DV * h:GDN_CONV_CH + GDN_DV * (h + 1)]
        y_h = o[h] * lax.rsqrt(ms[h] + NORM_EPS) * gn_ref[...] * _silu(g_h)
        o_ref[:, GDN_DV * h:GDN_DV * (h + 1)] = y_h.astype(o_ref.dtype)

    @pl.when(step == nc - 1)
    def _():
        sout_ref[...] = s_scr[...]


def _tri_consts(c):
    t = np.arange(c)
    tril = (t[:, None] >= t[None, :]).astype(np.float32)
    mlast = (t[None, :] > t[:, None]).astype(np.float32)
    triu_wide = np.pad(tril.T, ((0, 0), (0, LANES - c)))
    return jnp.asarray(tril, _BF), jnp.asarray(triu_wide, _BF), jnp.asarray(mlast, _BF)


def _gdn_part(zd, zs, cw, alog, dtb, gn, cache, s0, b, t, c, nc):
    tril, triu, mlast = _tri_consts(c)
    return _Part(
        functools.partial(_gdn_body, c=c, nc=nc),
        [zd.reshape(b, t, -1), zs.reshape(b, t, -1), cw, alog, dtb, gn, cache, s0, tril, triu, mlast],
        [_tok(c, ZD_W), _tok(c, ZS_W, ZS_COL), _full(cw), _full(alog), _full(dtb), _full(gn), _per_batch(cache),
         _per_batch(s0), _full(tril), _full(triu), _full(mlast)],
        [_tok(c, GDN_V), _per_batch(cache), _per_batch(s0)],
        [jax.ShapeDtypeStruct((b, t, GDN_V), _BF), jax.ShapeDtypeStruct(cache.shape, _F32),
         jax.ShapeDtypeStruct(s0.shape, _F32)],
        [((c + SUBLANES, GDN_CONV_CH), _F32), (s0.shape[1:], _F32)])


def _rw_body(*refs, c, nc, first):
    if first:
        (zr_ref, mu_ref, w0_ref, wup_ref, a0_ref, aup_ref, gup_ref, kk_ref, ka_ref, rk_ref, lng_ref, lnb_ref,
         cache_ref, s0_ref, tril_ref, mlast_ref, o_ref, vf_out_ref, shout_ref, sout_ref, sbuf, s_scr) = refs
    else:
        (zr_ref, vf_ref, v0_ref, vdown_ref, vup_ref, mu_ref, w0_ref, wup_ref, a0_ref, aup_ref, gup_ref, kk_ref,
         ka_ref, rk_ref, lng_ref, lnb_ref, cache_ref, s0_ref, tril_ref, mlast_ref, o_ref, shout_ref, sout_ref,
         sbuf, s_scr) = refs
    step = pl.program_id(1)
    top = SUBLANES

    @pl.when(step == 0)
    def _():
        sbuf[top - 1:top, :] = cache_ref[...]
        s_scr[...] = s0_ref[...]

    x = zr_ref[...]
    sbuf[top:top + c, :] = x
    zprev = sbuf[top - 1:top - 1 + c, :]
    last = sbuf[c + top - 1:c + top, :]
    sbuf[top - 1:top, :] = last

    @pl.when(step == nc - 1)
    def _():
        shout_ref[...] = last

    yield
    zm = x + (zprev - x) * mu_ref[...]
    xr, xk, xv = zm[:, 0:RW_C], zm[:, RW_C:2 * RW_C], zm[:, 2 * RW_C:3 * RW_C]
    xwa = zm[:, 3 * RW_C:3 * RW_C + LANES]
    xg = zm[:, 3 * RW_C + LANES:]
    wlog = -_softplus(-(w0_ref[...] + _dot(jnp.tanh(xwa), wup_ref[...]))) - 0.5
    lw = -jnp.exp(wlog)
    a = _sigmoid(a0_ref[...] + _dot(xwa, aup_ref[...]))
    yield
    if first:
        vf_out_ref[...] = xv
    else:
        nu = _sigmoid(v0_ref[...] + _dot(_dot(xv, vdown_ref[...]), vup_ref[...]))
        xv = xv + (vf_ref[...] - xv) * nu
    kkp = xk * kk_ref[...]
    xk2 = xk * (1.0 + (a - 1.0) * ka_ref[...])
    gate = _dot(_sigmoid(xg), gup_ref[...])
    cum = _cdot(tril_ref[...], lw)
    yield
    e_c = jnp.exp(cum)
    e_cp = jnp.exp(cum - lw)
    e_nc = jnp.exp(-cum)
    e_dl = jnp.exp(_cdot(mlast_ref[...], lw))
    half = _lane_halves((c, LANES))
    half2 = _lane_halves((2 * c, LANES))
    rows = lax.broadcasted_iota(jnp.int32, (c, LANES), 0)
    cols = lax.broadcasted_iota(jnp.int32, (c, LANES), 1) & (HALF - 1)
    strict = (rows > cols) & (cols < c)
    causal = (rows >= cols) & (cols < c)
    srow = lax.broadcasted_iota(jnp.int32, (LANES, LANES), 0) < (LANES // 2)
    scol = lax.broadcasted_iota(jnp.int32, (LANES, LANES), 1) < (LANES // 2)
    same_head = srow == scol
    ones = jnp.ones((c, LANES), _BF)
    pairs = range(RW_HEADS // 2)
    sls = [slice(LANES * p, LANES * (p + 1)) for p in pairs]
    r, k, v = [xr[:, s] for s in sls], [xk2[:, s] for s in sls], [xv[:, s] for s in sls]
    kap = [kkp[:, s] for s in sls]
    kap = [x * lax.rsqrt(_half_sum(x * x, half) + L2_EPS) for x in kap]
    ahat = [-(kap[p] * a[:, sls[p]]) for p in pairs]
    x2 = [jnp.concatenate([kap[p] * e_cp[:, sls[p]], r[p] * e_c[:, sls[p]]], axis=0) for p in pairs]
    at = [ahat[p] * e_nc[:, sls[p]] for p in pairs]
    kt = [k[p] * e_nc[:, sls[p]] for p in pairs]
    s_bd = [s_scr[p] for p in pairs]
    bsum = [_half_sum(r[p] * k[p] * rk_ref[:, sls[p]], half) for p in pairs]
    yield
    ps = [_dot(x2[p], s_bd[p]) for p in pairs]
    sc = []
    for p in pairs:
        x2b = x2[p].astype(_BF)
        both = jnp.concatenate([jnp.where(half2, x2b, 0), jnp.where(half2, 0, x2b)], axis=0)
        both = _dot_nt(both, _rows_pair(at[p], kt[p]))
        sc += [both[:2 * c], both[2 * c:]]
    yield
    ws = _inverse_start([jnp.where(half & strict, -x[:c], 0.0) for x in sc], c)
    akv = [_dot(jnp.concatenate([jnp.where(half, 0.0, jnp.where(strict, sc[2 * p + e][:c], 0.0)) for e in (0, 1)], axis=0),
                _rows_at(v[p], HALF)) for p in pairs]
    rhs = [ps[p][:c] + jnp.where(half, akv[p][:c], akv[p][c:]) for p in pairs]
    for _ in range(int(np.log2(c))):
        ws = _inverse_step(ws, c)
        yield
    u = [_solve_with(jnp.concatenate([ws[2 * p], ws[2 * p + 1]], axis=0), rhs[p]) for p in pairs]
    u = [jnp.where(half, x[:c], x[c:]) for x in u]
    yield
    oh = [_dot(jnp.concatenate([jnp.where(causal, sc[2 * p + e][c:], 0.0) for e in (0, 1)], axis=0),
               _rows_pair(u[p], v[p])) for p in pairs]
    o = [ps[p][c:] + jnp.where(half, oh[p][:c], oh[p][c:]) for p in pairs]
    yield
    mean = [_half_sum(o[p], half) * (1.0 / RW_N) for p in pairs]
    yield
    for p in pairs:
        dl = e_dl[:, sls[p]]
        upd = _dot_tn(jnp.concatenate([ahat[p] * dl, k[p] * dl], axis=0), jnp.concatenate([u[p], v[p]], axis=0))
        s_scr[p] = jnp.exp(_cdot_tn(lw[:, sls[p]], ones)) * s_bd[p] + jnp.where(same_head, upd, 0.0)
    dev = [o[p] - mean[p] for p in pairs]
    var = [_half_sum(dev[p] * dev[p], half) * (1.0 / RW_N) for p in pairs]
    yield
    for p in pairs:
        on = dev[p] * lax.rsqrt(var[p] + RW_GN_EPS) * lng_ref[:, sls[p]] + lnb_ref[:, sls[p]]
        o_ref[:, sls[p]] = ((on + bsum[p] * v[p]) * gate[:, sls[p]]).astype(o_ref.dtype)

    @pl.when(step == nc - 1)
    def _():
        sout_ref[...] = s_scr[...]


def _rw_part(zr, vf, vmix, prm, cache, s0, b, t, c, nc):
    tril, _, mlast = _tri_consts(c)
    first = vf is None
    args = ([zr.reshape(b, t, -1)] + ([] if first else [vf.reshape(b, t, -1)] + list(vmix)) + list(prm)
            + [cache, s0, tril, mlast])
    in_specs = ([_tok(c, RW_PROJ)] + ([] if first else [_tok(c, RW_C)] + [_full(a) for a in vmix])
                + [_full(a) for a in prm] + [_per_batch(cache), _per_batch(s0), _full(tril), _full(mlast)])
    out_specs = [_tok(c, RW_C)] + ([_tok(c, RW_C)] if first else []) + [_per_batch(cache), _per_batch(s0)]
    out_shape = ([jax.ShapeDtypeStruct((b, t, RW_C), _BF)] + ([jax.ShapeDtypeStruct((b, t, RW_C), _F32)] if first else [])
                 + [jax.ShapeDtypeStruct(cache.shape, _F32), jax.ShapeDtypeStruct(s0.shape, _F32)])
    return _Part(functools.partial(_rw_body, c=c, nc=nc, first=first), args, in_specs, out_specs, out_shape,
                 [((c + SUBLANES, RW_PROJ), _F32), (s0.shape[1:], _F32)])


def _prep_weights(p):
    bf = lambda a: a.astype(_BF)
    w_in = bf(p["w_in"])
    o1 = GLA_PROJ
    o2 = GLA_PROJ + GDN_PROJ
    lora0 = 2 * GLA_QK + GLA_V
    narrow = jnp.concatenate([w_in[:, :, lora0:lora0 + GLA_LORA],
                              w_in[:, :, o1 + GDN_CONV_CH:o1 + GDN_CONV_CH + 2 * GDN_HEADS]], axis=2)
    w_small = jnp.pad(narrow, ((0, 0), (0, 0), (0, ZS_W - narrow.shape[2])))
    w_gla = jnp.concatenate([w_in[:, :, 0:lora0], w_in[:, :, lora0 + GLA_LORA:o1], w_small], axis=2)
    w_gdn = jnp.concatenate([w_in[:, :, o1:o1 + GDN_CONV_CH], w_in[:, :, o1 + GDN_CONV_CH + 2 * GDN_HEADS:o2]], axis=2)
    w_rw = w_in[:, :, o2:]
    pad_rows = lambda a, lo, hi: jnp.pad(a, ((0, 0), (lo, hi), (0, 0)))
    pad_lanes = lambda a, lo: jnp.pad(a, ((0, 0), (lo, ZS_W - lo - a.shape[1])))[:, None, :]
    row = lambda a: a.reshape(a.shape[0], 1, -1)
    return dict(
        w_gla=w_gla, w_gdn=w_gdn, w_rw=w_rw,
        w_out=bf(p["w_out"]), w_g=bf(p["w_ffn_gate"]), w_u=bf(p["w_ffn_up"]), w_d=bf(p["w_ffn_down"]),
        gla_aup=bf(pad_rows(p["gla_a_up"], 0, ZS_W - GLA_LORA)), gla_abias=row(p["gla_a_bias"]),
        gla_gn=row(p["gla_norm_g"]),
        gdn_cw=p["gdn_conv_w"], gdn_alog=pad_lanes(p["gdn_A_log"], ZS_DT), gdn_dtb=pad_lanes(p["gdn_dt_bias"], ZS_DT),
        gdn_gn=row(p["gdn_norm_g"]),
        rw_mu=row(p["rw_mu"]), rw_w0=row(p["rw_w0"]), rw_wup=bf(pad_rows(p["rw_w_up"], 0, LANES - RW_DECAY_LORA)),
        rw_a0=row(p["rw_a0"]), rw_aup=bf(pad_rows(p["rw_a_up"], RW_DECAY_LORA, 0)), rw_gup=bf(p["rw_g_up"]),
        rw_kk=row(p["rw_k_k"]), rw_ka=row(p["rw_k_a"]), rw_rk=p["rw_r_k"].reshape(DEPTH, 1, RW_C),
        rw_lng=row(p["rw_ln_g"]), rw_lnb=row(p["rw_ln_b"]),
        rw_v0=row(p["rw_v0"]), rw_vdown=bf(p["rw_v_down"]), rw_vup=bf(p["rw_v_up"]),
        norm1=row(p["norm1_g"]), norm2=row(p["norm2_g"]), final=p["final_norm_g"].reshape(1, -1),
    )


def _pair_block_diag(s):
    b, h, n, _ = s.shape
    s = s.reshape(b, h // 2, 2, n, n)
    z = jnp.zeros_like(s[:, :, 0])
    top = jnp.concatenate([s[:, :, 0], z], axis=-1)
    bot = jnp.concatenate([z, s[:, :, 1]], axis=-1)
    return jnp.concatenate([top, bot], axis=-2)


def _pair_blocks(s):
    b, hp, n2, _ = s.shape
    n = n2 // 2
    return jnp.stack([s[:, :, :n, :n], s[:, :, n:, n:]], axis=2).reshape(b, 2 * hp, n, n)


def _trunk(x, s_gla, s_gdn, c_gdn, s_rw, c_rw, w):
    b, t, d = x.shape
    m = b * t
    tm = min(ROW_TILE, m)
    tp = min(PROJ_ROW_TILE, m)
    c = min(CHUNK, t)
    nc = t // c
    xf = x.reshape(m, d)
    h = _norm_call(xf, w["norm1"][0], tm)
    outs = ([], [], [], [], [])
    vf = None
    for l in range(DEPTH):
        zg = _mm_call(h, w["w_gla"], l, tp, ZG_W + ZS_W, "proj_gla")
        zd = _mm_call(h, w["w_gdn"], l, tp, ZD_W // 2, "proj_gdn")
        zr = _mm_call(h, w["w_rw"], l, tp, RW_PROJ // 2, "proj_rw")
        zs = zg
        prm = [w[n][l] for n in ("rw_mu", "rw_w0", "rw_wup", "rw_a0", "rw_aup", "rw_gup", "rw_kk", "rw_ka",
                                 "rw_rk", "rw_lng", "rw_lnb")]
        vmix = None if l == 0 else [w[n][l - 1] for n in ("rw_v0", "rw_vdown", "rw_vup")]
        parts = dict(
            rw=_rw_part(zr, vf, vmix, prm, c_rw[l], _pair_block_diag(s_rw[l]), b, t, c, nc),
            gdn=_gdn_part(zd, zs, w["gdn_cw"][l], w["gdn_alog"][l], w["gdn_dtb"][l], w["gdn_gn"][l],
                          c_gdn[l], s_gdn[l], b, t, c, nc),
            gla=_gla_part(zg, zs, w["gla_aup"][l], w["gla_abias"][l], w["gla_gn"][l],
                          s_gla[l].reshape(b, GLA_HEADS // 2, 2 * GLA_DK, GLA_DV), b, t, c, nc))
        outs_by = dict(zip(MIXER_ORDER, _mixers_call([parts[n] for n in MIXER_ORDER], b, nc)))
        res, (od, cg, sd), (og, sg) = outs_by["rw"], outs_by["gdn"], outs_by["gla"]
        if l == 0:
            orw, vf, cr, sr = res
        else:
            orw, cr, sr = res
        flat = lambda a: a.reshape(m, a.shape[-1])
        xf, h2 = _outproj_call(flat(og), flat(od), flat(orw), w["w_out"], l, xf, w["norm2"][l], tm)
        last = l == DEPTH - 1
        g_next = w["final"] if last else w["norm1"][l + 1]
        res = _ffn_call(h2, w["w_g"], w["w_u"], w["w_d"], l, xf, g_next, tm, FFN_TF, last)
        if last:
            (y,) = res
        else:
            xf, h = res
        for lst, arr in zip(outs, (sg.reshape(b, GLA_HEADS, GLA_DK, GLA_DV), sd, cg, _pair_blocks(sr), cr)):
            lst.append(arr)
    return y.reshape(b, t, d), [jnp.stack(lst) for lst in outs]


def kernel(x_prompt, x_sample, state_gla, state_gdn, cache_gdn_conv, state_rwkv, cache_rwkv_shift, norm1_g, w_in, gla_a_up, gla_a_bias, gla_norm_g, gdn_conv_w, gdn_A_log, gdn_dt_bias, gdn_norm_g, rw_mu, rw_w0, rw_w_up, rw_a0, rw_a_up, rw_v0, rw_v_down, rw_v_up, rw_g_up, rw_k_k, rw_k_a, rw_r_k, rw_ln_g, rw_ln_b, w_out, norm2_g, w_ffn_gate, w_ffn_up, w_ffn_down, final_norm_g):
    p = dict(norm1_g=norm1_g, w_in=w_in, gla_a_up=gla_a_up, gla_a_bias=gla_a_bias, gla_norm_g=gla_norm_g,
             gdn_conv_w=gdn_conv_w, gdn_A_log=gdn_A_log, gdn_dt_bias=gdn_dt_bias, gdn_norm_g=gdn_norm_g,
             rw_mu=rw_mu, rw_w0=rw_w0, rw_w_up=rw_w_up, rw_a0=rw_a0, rw_a_up=rw_a_up, rw_v0=rw_v0,
             rw_v_down=rw_v_down, rw_v_up=rw_v_up, rw_g_up=rw_g_up, rw_k_k=rw_k_k, rw_k_a=rw_k_a,
             rw_r_k=rw_r_k, rw_ln_g=rw_ln_g, rw_ln_b=rw_ln_b, w_out=w_out, norm2_g=norm2_g,
             w_ffn_gate=w_ffn_gate, w_ffn_up=w_ffn_up, w_ffn_down=w_ffn_down, final_norm_g=final_norm_g)
    w = _prep_weights(p)
    bp = x_prompt.shape[0]
    zeros = lambda a: jnp.zeros((DEPTH, bp) + a.shape[2:], a.dtype)
    y_p, st_p = _trunk(x_prompt, zeros(state_gla), zeros(state_gdn), zeros(cache_gdn_conv), zeros(state_rwkv),
                       zeros(cache_rwkv_shift), w)
    y_s, st_s = _trunk(x_sample, state_gla, state_gdn, cache_gdn_conv, state_rwkv, cache_rwkv_shift, w)
    return (y_p, y_s, *st_p, *st_s)
```

```python
import functools

import numpy as np
import jax
import jax.numpy as jnp
from jax import lax
from jax.experimental import pallas as pl
from jax.experimental.pallas import tpu as pltpu

D_MODEL = 2048
DEPTH = 4
CHUNK = 64
NORM_EPS = 1e-6
L2_EPS = 1e-6
GLA_HEADS, GLA_DK, GLA_DV, GLA_LORA, GLA_GATE_TEMP = 4, 64, 128, 16, 16.0
GDN_HEADS, GDN_DK, GDN_DV, CONV_W = 6, 128, 128, 4
RW_HEADS, RW_N = 12, 64
RW_DECAY_LORA, RW_AAA_LORA, RW_MV_LORA, RW_GATE_LORA = 64, 64, 32, 128
RW_GN_EPS = 64e-5

GLA_QK = GLA_HEADS * GLA_DK
GLA_V = GLA_HEADS * GLA_DV
GDN_QK = GDN_HEADS * GDN_DK
GDN_V = GDN_HEADS * GDN_DV
GDN_CONV_CH = 2 * GDN_QK + GDN_V
RW_C = RW_HEADS * RW_N
GLA_PROJ = 2 * GLA_QK + GLA_V + GLA_LORA + GLA_V
GDN_PROJ = GDN_CONV_CH + 2 * GDN_HEADS + GDN_V
RW_PROJ = 3 * RW_C + RW_DECAY_LORA + RW_AAA_LORA + RW_GATE_LORA
D_FF = -(-8 * D_MODEL // (3 * 256)) * 256

LANES = 128
SUBLANES = 8
ZG_W = 2 * GLA_QK + 2 * GLA_V
ZD_W = GDN_CONV_CH + GDN_V
ZS_W = LANES
ZS_BETA = GLA_LORA
ZS_DT = GLA_LORA + GDN_HEADS
VMEM_LIMIT = 52 * 1024 * 1024
ROW_TILE = 512
PROJ_ROW_TILE = 1024
FFN_TF = 512

_BF = jnp.bfloat16
_F32 = jnp.float32
_NEG = -1e30


def _dot(a, b):
    return jnp.dot(a.astype(_BF), b.astype(_BF), preferred_element_type=_F32)


def _dot_nt(a, b):
    return lax.dot_general(a.astype(_BF), b.astype(_BF), (((1,), (1,)), ((), ())),
                           preferred_element_type=_F32)


def _dot_tn(a, b):
    return lax.dot_general(a.astype(_BF), b.astype(_BF), (((0,), (0,)), ((), ())),
                           preferred_element_type=_F32)


def _split2(x):
    hi = x.astype(_BF)
    lo = (x - hi.astype(_F32)).astype(_BF)
    return hi, lo


def _cdot(cc, x):
    return jnp.dot(cc, jnp.concatenate(_split2(x), axis=0), preferred_element_type=_F32)


def _cdot_tn(x, c):
    n = x.shape[1]
    y = lax.dot_general(jnp.concatenate(_split2(x), axis=1), c, (((0,), (0,)), ((), ())),
                        preferred_element_type=_F32)
    return y[:n] + y[n:]


def _sigmoid(x):
    return 1.0 / (1.0 + jnp.exp(-x))


def _silu(x):
    return x * _sigmoid(x)


def _softplus(x):
    return jnp.maximum(x, 0.0) + jnp.log(1.0 + jnp.exp(-jnp.abs(x)))


HALF = LANES // 2


def _rows_at(x, first, total=LANES):
    n, w = x.shape
    parts = [jnp.zeros((first, w), x.dtype)] if first else []
    parts.append(x)
    if total - first - n:
        parts.append(jnp.zeros((total - first - n, w), x.dtype))
    return jnp.concatenate(parts, axis=0) if len(parts) > 1 else x


def _rows_pair(a, b):
    return jnp.concatenate([_rows_at(a, 0, HALF), _rows_at(b, 0, HALF)], axis=0)


def _inverse_start(lows, n):
    rows = lax.broadcasted_iota(jnp.int32, (n, LANES), 0)
    lane = lax.broadcasted_iota(jnp.int32, (n, LANES), 1)
    return [jnp.where(lane == rows + HALF, 1.0, 0.0) - low for low in lows]


def _inverse_step(ws, n):
    keep = lax.broadcasted_iota(jnp.int32, (n, LANES), 1) >= HALF
    wbs = [w.astype(_BF) for w in ws]
    ys = [jnp.dot(wb, _rows_at(wb, 0), preferred_element_type=_F32) for wb in wbs]
    return [y + jnp.where(keep, w, 0.0) for y, w in zip(ys, ws)]


def _solve_with(w, rhs):
    n = rhs.shape[1]
    both = jnp.concatenate([_rows_at(part, HALF) for part in _split2(rhs)], axis=1)
    y = jnp.dot(w.astype(_BF), both, preferred_element_type=_F32)
    return y[:, :n] + y[:, n:]


def _lane_halves(shape):
    lane = lax.broadcasted_iota(jnp.int32, shape, len(shape) - 1)
    return lane < (LANES // 2)


def _half_sum(x, half):
    s0 = jnp.sum(jnp.where(half, x, 0.0), axis=-1, keepdims=True)
    s1 = jnp.sum(jnp.where(half, 0.0, x), axis=-1, keepdims=True)
    return jnp.where(half, s0, s1)


def _rmsnorm_rows(x, g):
    return x * lax.rsqrt(jnp.mean(x * x, axis=-1, keepdims=True) + NORM_EPS) * g


def _params(*sem):
    return pltpu.CompilerParams(dimension_semantics=sem, vmem_limit_bytes=VMEM_LIMIT)


def _norm_kernel(x_ref, g_ref, o_ref):
    o_ref[...] = _rmsnorm_rows(x_ref[...], g_ref[...]).astype(o_ref.dtype)


def _norm_call(x, g, tm):
    m, d = x.shape
    return pl.pallas_call(
        _norm_kernel,
        grid=(m // tm,),
        in_specs=[pl.BlockSpec((tm, d), lambda i: (i, 0)), pl.BlockSpec((1, d), lambda i: (0, 0))],
        out_specs=pl.BlockSpec((tm, d), lambda i: (i, 0)),
        out_shape=jax.ShapeDtypeStruct((m, d), _BF),
        compiler_params=_params("arbitrary"),
        name="rmsnorm",
    )(x, g)


def _mm_kernel(a_ref, w_ref, o_ref):
    o_ref[...] = jnp.dot(a_ref[...], w_ref[...], preferred_element_type=_F32)


def _mm_call(a, w, l, tm, tn, name):
    m, k = a.shape
    n = w.shape[2]
    return pl.pallas_call(
        _mm_kernel,
        grid=(n // tn, m // tm),
        in_specs=[pl.BlockSpec((tm, k), lambda j, i: (i, 0)), pl.BlockSpec((None, k, tn), lambda j, i: (l, 0, j))],
        out_specs=pl.BlockSpec((tm, tn), lambda j, i: (i, j)),
        out_shape=jax.ShapeDtypeStruct((m, n), _F32),
        compiler_params=_params("arbitrary", "arbitrary"),
        name=name,
    )(a, w)


def _outproj_kernel(og_ref, od_ref, or_ref, w_ref, x_ref, g_ref, x1_ref, h_ref):
    d = functools.partial(jnp.dot, preferred_element_type=_F32)
    acc = (d(og_ref[...], w_ref[0:GLA_V, :]) + d(od_ref[...], w_ref[GLA_V:GLA_V + GDN_V, :])
           + d(or_ref[...], w_ref[GLA_V + GDN_V:, :]))
    x1 = x_ref[...] + acc
    x1_ref[...] = x1
    h_ref[...] = _rmsnorm_rows(x1, g_ref[...]).astype(h_ref.dtype)


def _outproj_call(og, od, orw, w, l, x, g, tm):
    m, d = x.shape
    row = lambda width: pl.BlockSpec((tm, width), lambda i: (i, 0))
    return pl.pallas_call(
        _outproj_kernel,
        grid=(m // tm,),
        in_specs=[row(GLA_V), row(GDN_V), row(RW_C), pl.BlockSpec((None,) + w.shape[1:], lambda i: (l, 0, 0)), row(d),
                  pl.BlockSpec((1, d), lambda i: (0, 0))],
        out_specs=[row(d), row(d)],
        out_shape=[jax.ShapeDtypeStruct((m, d), _F32), jax.ShapeDtypeStruct((m, d), _BF)],
        compiler_params=_params("arbitrary"),
        name="outproj",
    )(og, od, orw, w, x, g)


def _ffn_kernel(h_ref, wg_ref, wu_ref, wd_ref, x_ref, g_ref, *rest, nf, emit_x):
    if emit_x:
        x2_ref, hn_ref, acc_ref = rest
    else:
        hn_ref, acc_ref = rest
    f = pl.program_id(1)

    def tile():
        h = h_ref[...]
        gate = jnp.dot(h, wg_ref[...], preferred_element_type=_F32)
        up = jnp.dot(h, wu_ref[...], preferred_element_type=_F32)
        act = (_silu(gate) * up).astype(_BF)
        return jnp.dot(act, wd_ref[...], preferred_element_type=_F32)

    @pl.when(f == 0)
    def _():
        acc_ref[...] = tile()

    @pl.when((f > 0) & (f < nf - 1))
    def _():
        acc_ref[...] += tile()

    @pl.when(f == nf - 1)
    def _():
        x2 = x_ref[...] + (acc_ref[...] + tile())
        if emit_x:
            x2_ref[...] = x2
        hn_ref[...] = _rmsnorm_rows(x2, g_ref[...]).astype(hn_ref.dtype)


def _ffn_call(h, wg, wu, wd, l, x, g, tm, tf, last):
    m, d = x.shape
    nf = D_FF // tf
    assert nf >= 2 and nf * tf == D_FF
    row = pl.BlockSpec((tm, d), lambda i, f: (i, 0))
    if last:
        out_specs = [row]
        out_shape = [jax.ShapeDtypeStruct((m, d), _F32)]
    else:
        out_specs = [row, row]
        out_shape = [jax.ShapeDtypeStruct((m, d), _F32), jax.ShapeDtypeStruct((m, d), _BF)]
    return pl.pallas_call(
        functools.partial(_ffn_kernel, nf=nf, emit_x=not last),
        grid=(m // tm, nf),
        in_specs=[row, pl.BlockSpec((None, d, tf), lambda i, f: (l, 0, f)),
                  pl.BlockSpec((None, d, tf), lambda i, f: (l, 0, f)),
                  pl.BlockSpec((None, tf, d), lambda i, f: (l, f, 0)), row, pl.BlockSpec((1, d), lambda i, f: (0, 0))],
        out_specs=out_specs,
        out_shape=out_shape,
        scratch_shapes=[pltpu.VMEM((tm, d), _F32)],
        compiler_params=_params("arbitrary", "arbitrary"),
        name="ffn",
    )(h, wg, wu, wd, x, g)


def _gla_consts(c):
    nlev = int(np.log2(c))
    t = np.arange(c)
    blocks = [t[:, None] >= t[None, :],
              t[None, :] > t[:, None]]
    mq, mk, masks = [], [], []
    for lv in range(nlev):
        m = c >> lv
        half = m // 2
        blk, pos = t // m, t % m
        mid = blk * m + half
        upper = pos >= half
        mq.append(upper[:, None] & (t[None, :] >= mid[:, None]) & (t[None, :] <= t[:, None]))
        mk.append((~upper)[:, None] & (t[None, :] > t[:, None]) & (t[None, :] <= mid[:, None] - 1))
        masks.append((blk[:, None] == blk[None, :]) & upper[:, None] & (~upper)[None, :])
    mc = np.concatenate(blocks + mq + mk, axis=0).astype(np.float32)
    return jnp.asarray(np.tile(mc, (1, 2)), _BF), jnp.asarray(np.stack(masks).astype(np.float32)), nlev


def _gla_body(zg_ref, zs_ref, aup_ref, abias_ref, gn_ref, s0_ref, mc_ref, mk_ref,
              o_ref, sout_ref, s_scr, *, c, nlev, nc):
    step = pl.program_id(1)

    @pl.when(step == 0)
    def _():
        s_scr[...] = s0_ref[...]

    xa = _dot(zs_ref[...], aup_ref[...]) + abias_ref[...]
    la = (jnp.minimum(xa, 0.0) - jnp.log(1.0 + jnp.exp(-jnp.abs(xa)))) * (1.0 / GLA_GATE_TEMP)
    yield
    ex = jnp.exp(_cdot(mc_ref[...], la))
    yield
    q = zg_ref[:, 0:GLA_QK] * (GLA_DK ** -0.5)
    k = zg_ref[:, GLA_QK:2 * GLA_QK]
    qd = q * ex[0:c]
    kd = k * ex[c:2 * c]
    half = _lane_halves((c, LANES))
    eye = lax.broadcasted_iota(jnp.int32, (c, c), 0) == lax.broadcasted_iota(jnp.int32, (c, c), 1)
    ones = jnp.ones((c, LANES), _BF)
    heads = range(GLA_HEADS)
    sls = [slice(LANES * (h // 2), LANES * (h // 2 + 1)) for h in heads]
    sel = [(lambda x: jnp.where(half, x, 0.0)) if h % 2 == 0 else (lambda x: jnp.where(half, 0.0, x)) for h in heads]
    v = [zg_ref[:, 2 * GLA_QK + GLA_DV * h:2 * GLA_QK + GLA_DV * (h + 1)] for h in heads]
    s_old = [s_scr[p] for p in range(GLA_HEADS // 2)]
    prs = range(GLA_HEADS // 2)
    psl = [sls[2 * p] for p in prs]
    both = lambda x: jnp.concatenate([sel[0](x), sel[1](x)], axis=0)
    split = lambda xs: [x[e * c:(e + 1) * c] for x in xs for e in (0, 1)]
    qk = q * k
    a = [jnp.where(eye, jnp.sum(sel[h](qk[:, sls[h]]), axis=-1, keepdims=True), 0.0) for h in heads]
    yield
    for lv in range(nlev):
        exq = ex[(2 + lv) * c:(3 + lv) * c]
        kx = k * ex[(2 + nlev + lv) * c:(3 + nlev + lv) * c]
        qx = q * exq
        lvl = split([_dot_nt(both(qx[:, psl[p]]), kx[:, psl[p]]) for p in prs])
        a = [a[h] + mk_ref[lv] * lvl[h] for h in heads]
        yield
    inter = split([_dot(both(qd[:, psl[p]]), s_old[p]) for p in prs])
    o = [inter[h] + _dot(a[h], v[h]) for h in heads]
    yield
    ms = [jnp.mean(o[h] * o[h], axis=-1, keepdims=True) for h in heads]
    for p in range(GLA_HEADS // 2):
        sl = sls[2 * p]
        s_scr[p] = (jnp.exp(_cdot_tn(la[:, sl], ones)) * s_old[p]
                    + _dot_tn(both(kd[:, sl]), jnp.concatenate([v[2 * p], v[2 * p + 1]], axis=0)))
    yield
    for h in heads:
        g_h = zg_ref[:, 2 * GLA_QK + GLA_V + GLA_DV * h:2 * GLA_QK + GLA_V + GLA_DV * (h + 1)]
        y = o[h] * lax.rsqrt(ms[h] + NORM_EPS) * gn_ref[...] * _silu(g_h)
        o_ref[:, GLA_DV * h:GLA_DV * (h + 1)] = y.astype(o_ref.dtype)

    @pl.when(step == nc - 1)
    def _():
        sout_ref[...] = s_scr[...]


MIXER_ORDER = ("rw", "gdn", "gla")


class _Part:
    def __init__(self, body, args, in_specs, out_specs, out_shape, scratch):
        self.body, self.args, self.in_specs = body, list(args), list(in_specs)
        self.out_specs, self.out_shape = list(out_specs), list(out_shape)
        self.scratch = [pltpu.VMEM(shape, dtype) for shape, dtype in scratch]


def _full(a):
    return pl.BlockSpec(a.shape, lambda i, j: (0,) * a.ndim)


def _per_batch(a):
    return pl.BlockSpec((None,) + a.shape[1:], lambda i, j: (i,) + (0,) * (a.ndim - 1))


def _tok(c, w, col=0):
    return pl.BlockSpec((None, c, w), lambda i, j: (i, j, col))


ZS_COL = ZG_W // ZS_W


def _gla_part(zg, zs, aup, abias, gn, s0, b, t, c, nc):
    mc, masks, nlev = _gla_consts(c)
    return _Part(
        functools.partial(_gla_body, c=c, nlev=nlev, nc=nc),
        [zg.reshape(b, t, -1), zs.reshape(b, t, -1), aup, abias, gn, s0, mc, masks],
        [_tok(c, ZG_W), _tok(c, ZS_W, ZS_COL), _full(aup), _full(abias), _full(gn), _per_batch(s0), _full(mc),
         _full(masks)],
        [_tok(c, GLA_V), _per_batch(s0)],
        [jax.ShapeDtypeStruct((b, t, GLA_V), _BF), jax.ShapeDtypeStruct(s0.shape, _F32)],
        [(s0.shape[1:], _F32)])


def _mixers_kernel(*refs, bodies, counts):
    n_in, n_out = sum(x[0] for x in counts), sum(x[1] for x in counts)
    i0, o0, s0 = 0, n_in, n_in + n_out
    gens = []
    for body, (ni, no, ns) in zip(bodies, counts):
        gens.append(body(*refs[i0:i0 + ni], *refs[o0:o0 + no], *refs[s0:s0 + ns]))
        i0, o0, s0 = i0 + ni, o0 + no, s0 + ns
    while gens:
        gens = [g for g in gens if next(g, StopIteration) is not StopIteration]


def _mixers_call(parts, b, nc):
    outs = pl.pallas_call(
        functools.partial(_mixers_kernel, bodies=[p.body for p in parts],
                          counts=[(len(p.args), len(p.out_shape), len(p.scratch)) for p in parts]),
        grid=(b, nc),
        in_specs=[s for p in parts for s in p.in_specs],
        out_specs=[s for p in parts for s in p.out_specs],
        out_shape=[s for p in parts for s in p.out_shape],
        scratch_shapes=[s for p in parts for s in p.scratch],
        compiler_params=_params("arbitrary", "arbitrary"),
        name="mixers",
    )(*[a for p in parts for a in p.args])
    res, k = [], 0
    for p in parts:
        res.append(outs[k:k + len(p.out_shape)])
        k += len(p.out_shape)
    return res


def _gdn_body(zd_ref, zs_ref, cw_ref, alog_ref, dtb_ref, gn_ref, cache_ref, s0_ref, tril_ref, triu_ref, mlast_ref,
              o_ref, cout_ref, sout_ref, cbuf, s_scr, *, c, nc):
    step = pl.program_id(1)
    top = SUBLANES
    tail0 = top - (CONV_W - 1)

    @pl.when(step == 0)
    def _():
        cbuf[tail0:top, :] = cache_ref[...]
        s_scr[...] = s0_ref[...]

    x = zd_ref[:, 0:GDN_CONV_CH]
    cbuf[top:top + c, :] = x
    y = cbuf[tail0:tail0 + c, :] * cw_ref[0:1, :]
    for j in range(1, CONV_W - 1):
        y = y + cbuf[tail0 + j:tail0 + j + c, :] * cw_ref[j:j + 1, :]
    y = y + x * cw_ref[CONV_W - 1:CONV_W, :]
    tail = cbuf[c + tail0:c + top, :]
    cbuf[tail0:top, :] = tail

    @pl.when(step == nc - 1)
    def _():
        cout_ref[...] = tail

    yield
    ys = _silu(y)
    yield
    zs = zs_ref[...]
    lg = -jnp.exp(alog_ref[...]) * _softplus(zs + dtb_ref[...])
    bt = _sigmoid(zs)
    bcum = _cdot(tril_ref[...], lg)
    bcum_t = _cdot_tn(lg, triu_ref[...])
    bdl = _cdot(mlast_ref[...], lg)
    btot_t = _cdot_tn(lg, jnp.ones((c, LANES), _BF))
    lane = lax.broadcasted_iota(jnp.int32, (c, LANES), 1)
    rows = lax.broadcasted_iota(jnp.int32, (c, LANES), 0)
    causal = (rows >= lane) & (lane < c)
    strict = (rows > lane) & (lane < c)
    heads = range(GDN_HEADS)
    pick = lambda tile, j: jnp.sum(jnp.where(lane == j, tile, 0.0), axis=1, keepdims=True)
    bcol = [pick(bcum, ZS_DT + h) for h in heads]
    beta = [pick(bt, ZS_BETA + h) for h in heads]
    dlcol = [pick(bdl, ZS_DT + h) for h in heads]
    dec = [jnp.exp(jnp.where(causal, bcol[h] - bcum_t[ZS_DT + h:ZS_DT + h + 1, :], _NEG)) for h in heads]
    eb = [jnp.exp(bcol[h]) for h in heads]
    yield
    q, k, v = [], [], []
    for h in heads:
        q_h = ys[:, GDN_DK * h:GDN_DK * (h + 1)]
        k_h = ys[:, GDN_QK + GDN_DK * h:GDN_QK + GDN_DK * (h + 1)]
        q.append(q_h * lax.rsqrt(jnp.sum(q_h * q_h, axis=-1, keepdims=True) + L2_EPS) * (GDN_DK ** -0.5))
        k.append(k_h * lax.rsqrt(jnp.sum(k_h * k_h, axis=-1, keepdims=True) + L2_EPS))
        v.append(ys[:, 2 * GDN_QK + GDN_DV * h:2 * GDN_QK + GDN_DV * (h + 1)])
    qk = [jnp.concatenate([q[h], k[h]], axis=0) for h in heads]
    s_old = [s_scr[h] for h in heads]
    yield
    sc = [_dot_nt(qk[h], _rows_at(k[h], 0)) for h in heads]
    ps = [_dot(qk[h], s_old[h]) for h in heads]
    yield
    ws = _inverse_start([jnp.where(strict, beta[h] * dec[h] * sc[h][c:], 0.0) for h in heads], c)
    for _ in range(int(np.log2(c))):
        ws = _inverse_step(ws, c)
        yield
    delta = [_solve_with(ws[h], beta[h] * (v[h] - eb[h] * ps[h][c:])) for h in heads]
    yield
    o = [eb[h] * ps[h][:c] + _dot(sc[h][:c] * dec[h], _rows_at(delta[h], 0)) for h in heads]
    yield
    ms = [jnp.mean(o[h] * o[h], axis=-1, keepdims=True) for h in heads]
    for h in heads:
        elast = jnp.exp(btot_t[ZS_DT + h:ZS_DT + h + 1, :])
        s_scr[h] = elast * s_old[h] + _dot_tn(k[h] * jnp.exp(dlcol[h]), delta[h])
    yield
    for h in heads:
        g_h = zd_ref[:, GDN_CONV_CH + GDN_DV * h:GDN_CONV_CH + GDN_DV * (h + 1)]
        y_h = o[h] * lax.rsqrt(ms[h] + NORM_EPS) * gn_ref[...] * _silu(g_h)
        o_ref[:, GDN_DV * h:GDN_DV * (h + 1)] = y_h.astype(o_ref.dtype)

    @pl.when(step == nc - 1)
    def _():
        sout_ref[...] = s_scr[...]


def _tri_consts(c):
    t = np.arange(c)
    tril = (t[:, None] >= t[None, :]).astype(np.float32)
    mlast = (t[None, :] > t[:, None]).astype(np.float32)
    triu_wide = np.pad(tril.T, ((0, 0), (0, LANES - c)))
    twice = lambda a: jnp.asarray(np.tile(a, (1, 2)), _BF)
    return twice(tril), jnp.asarray(triu_wide, _BF), twice(mlast)


def _gdn_part(zd, zs, cw, alog, dtb, gn, cache, s0, b, t, c, nc):
    tril, triu, mlast = _tri_consts(c)
    return _Part(
        functools.partial(_gdn_body, c=c, nc=nc),
        [zd.reshape(b, t, -1), zs.reshape(b, t, -1), cw, alog, dtb, gn, cache, s0, tril, triu, mlast],
        [_tok(c, ZD_W), _tok(c, ZS_W, ZS_COL), _full(cw), _full(alog), _full(dtb), _full(gn), _per_batch(cache),
         _per_batch(s0), _full(tril), _full(triu), _full(mlast)],
        [_tok(c, GDN_V), _per_batch(cache), _per_batch(s0)],
        [jax.ShapeDtypeStruct((b, t, GDN_V), _BF), jax.ShapeDtypeStruct(cache.shape, _F32),
         jax.ShapeDtypeStruct(s0.shape, _F32)],
        [((c + SUBLANES, GDN_CONV_CH), _F32), (s0.shape[1:], _F32)])


def _rw_body(*refs, c, nc, first):
    if first:
        (zr_ref, mu_ref, w0_ref, wup_ref, a0_ref, aup_ref, gup_ref, kk_ref, ka_ref, rk_ref, lng_ref, lnb_ref,
         cache_ref, s0_ref, tril_ref, mlast_ref, o_ref, vf_out_ref, shout_ref, sout_ref, sbuf, s_scr) = refs
    else:
        (zr_ref, vf_ref, v0_ref, vdown_ref, vup_ref, mu_ref, w0_ref, wup_ref, a0_ref, aup_ref, gup_ref, kk_ref,
         ka_ref, rk_ref, lng_ref, lnb_ref, cache_ref, s0_ref, tril_ref, mlast_ref, o_ref, shout_ref, sout_ref,
         sbuf, s_scr) = refs
    step = pl.program_id(1)
    top = SUBLANES

    @pl.when(step == 0)
    def _():
        sbuf[top - 1:top, :] = cache_ref[...]
        s_scr[...] = s0_ref[...]

    x = zr_ref[...]
    sbuf[top:top + c, :] = x
    zprev = sbuf[top - 1:top - 1 + c, :]
    last = sbuf[c + top - 1:c + top, :]
    sbuf[top - 1:top, :] = last

    @pl.when(step == nc - 1)
    def _():
        shout_ref[...] = last

    yield
    zm = x + (zprev - x) * mu_ref[...]
    xr, xk, xv = zm[:, 0:RW_C], zm[:, RW_C:2 * RW_C], zm[:, 2 * RW_C:3 * RW_C]
    xwa = zm[:, 3 * RW_C:3 * RW_C + LANES]
    xg = zm[:, 3 * RW_C + LANES:]
    wlog = -_softplus(-(w0_ref[...] + _dot(jnp.tanh(xwa), wup_ref[...]))) - 0.5
    lw = -jnp.exp(wlog)
    a = _sigmoid(a0_ref[...] + _dot(xwa, aup_ref[...]))
    yield
    if first:
        vf_out_ref[...] = xv
    else:
        nu = _sigmoid(v0_ref[...] + _dot(_dot(xv, vdown_ref[...]), vup_ref[...]))
        xv = xv + (vf_ref[...] - xv) * nu
    kkp = xk * kk_ref[...]
    xk2 = xk * (1.0 + (a - 1.0) * ka_ref[...])
    gate = _dot(_sigmoid(xg), gup_ref[...])
    cum = _cdot(tril_ref[...], lw)
    yield
    e_c = jnp.exp(cum)
    e_cp = jnp.exp(cum - lw)
    e_nc = jnp.exp(-cum)
    e_dl = jnp.exp(_cdot(mlast_ref[...], lw))
    half = _lane_halves((c, LANES))
    half2 = _lane_halves((2 * c, LANES))
    rows = lax.broadcasted_iota(jnp.int32, (c, LANES), 0)
    cols = lax.broadcasted_iota(jnp.int32, (c, LANES), 1) & (HALF - 1)
    strict = (rows > cols) & (cols < c)
    causal = (rows >= cols) & (cols < c)
    srow = lax.broadcasted_iota(jnp.int32, (LANES, LANES), 0) < (LANES // 2)
    scol = lax.broadcasted_iota(jnp.int32, (LANES, LANES), 1) < (LANES // 2)
    same_head = srow == scol
    ones = jnp.ones((c, LANES), _BF)
    pairs = range(RW_HEADS // 2)
    sls = [slice(LANES * p, LANES * (p + 1)) for p in pairs]
    r, k, v = [xr[:, s] for s in sls], [xk2[:, s] for s in sls], [xv[:, s] for s in sls]
    kap = [kkp[:, s] for s in sls]
    kap = [x * lax.rsqrt(_half_sum(x * x, half) + L2_EPS) for x in kap]
    ahat = [-(kap[p] * a[:, sls[p]]) for p in pairs]
    x2 = [jnp.concatenate([kap[p] * e_cp[:, sls[p]], r[p] * e_c[:, sls[p]]], axis=0) for p in pairs]
    at = [ahat[p] * e_nc[:, sls[p]] for p in pairs]
    kt = [k[p] * e_nc[:, sls[p]] for p in pairs]
    s_bd = [s_scr[p] for p in pairs]
    bsum = [_half_sum(r[p] * k[p] * rk_ref[:, sls[p]], half) for p in pairs]
    yield
    ps = [_dot(x2[p], s_bd[p]) for p in pairs]
    sc = []
    for p in pairs:
        x2b = x2[p].astype(_BF)
        both = jnp.concatenate([jnp.where(half2, x2b, 0), jnp.where(half2, 0, x2b)], axis=0)
        both = _dot_nt(both, _rows_pair(at[p], kt[p]))
        sc += [both[:2 * c], both[2 * c:]]
    yield
    ws = _inverse_start([jnp.where(half & strict, -x[:c], 0.0) for x in sc], c)
    akv = [_dot(jnp.concatenate([jnp.where(half, 0.0, jnp.where(strict, sc[2 * p + e][:c], 0.0)) for e in (0, 1)], axis=0),
                _rows_at(v[p], HALF)) for p in pairs]
    rhs = [ps[p][:c] + jnp.where(half, akv[p][:c], akv[p][c:]) for p in pairs]
    for _ in range(int(np.log2(c))):
        ws = _inverse_step(ws, c)
        yield
    u = [_solve_with(jnp.concatenate([ws[2 * p], ws[2 * p + 1]], axis=0), rhs[p]) for p in pairs]
    u = [jnp.where(half, x[:c], x[c:]) for x in u]
    yield
    oh = [_dot(jnp.concatenate([jnp.where(causal, sc[2 * p + e][c:], 0.0) for e in (0, 1)], axis=0),
               _rows_pair(u[p], v[p])) for p in pairs]
    o = [ps[p][c:] + jnp.where(half, oh[p][:c], oh[p][c:]) for p in pairs]
    yield
    mean = [_half_sum(o[p], half) * (1.0 / RW_N) for p in pairs]
    yield
    for p in pairs:
        dl = e_dl[:, sls[p]]
        upd = _dot_tn(jnp.concatenate([ahat[p] * dl, k[p] * dl], axis=0), jnp.concatenate([u[p], v[p]], axis=0))
        s_scr[p] = jnp.exp(_cdot_tn(lw[:, sls[p]], ones)) * s_bd[p] + jnp.where(same_head, upd, 0.0)
    dev = [o[p] - mean[p] for p in pairs]
    var = [_half_sum(dev[p] * dev[p], half) * (1.0 / RW_N) for p in pairs]
    yield
    for p in pairs:
        on = dev[p] * lax.rsqrt(var[p] + RW_GN_EPS) * lng_ref[:, sls[p]] + lnb_ref[:, sls[p]]
        o_ref[:, sls[p]] = ((on + bsum[p] * v[p]) * gate[:, sls[p]]).astype(o_ref.dtype)

    @pl.when(step == nc - 1)
    def _():
        sout_ref[...] = s_scr[...]


def _rw_part(zr, vf, vmix, prm, cache, s0, b, t, c, nc):
    tril, _, mlast = _tri_consts(c)
    first = vf is None
    args = ([zr.reshape(b, t, -1)] + ([] if first else [vf.reshape(b, t, -1)] + list(vmix)) + list(prm)
            + [cache, s0, tril, mlast])
    in_specs = ([_tok(c, RW_PROJ)] + ([] if first else [_tok(c, RW_C)] + [_full(a) for a in vmix])
                + [_full(a) for a in prm] + [_per_batch(cache), _per_batch(s0), _full(tril), _full(mlast)])
    out_specs = [_tok(c, RW_C)] + ([_tok(c, RW_C)] if first else []) + [_per_batch(cache), _per_batch(s0)]
    out_shape = ([jax.ShapeDtypeStruct((b, t, RW_C), _BF)] + ([jax.ShapeDtypeStruct((b, t, RW_C), _F32)] if first else [])
                 + [jax.ShapeDtypeStruct(cache.shape, _F32), jax.ShapeDtypeStruct(s0.shape, _F32)])
    return _Part(functools.partial(_rw_body, c=c, nc=nc, first=first), args, in_specs, out_specs, out_shape,
                 [((c + SUBLANES, RW_PROJ), _F32), (s0.shape[1:], _F32)])


def _prep_weights(p):
    bf = lambda a: a.astype(_BF)
    w_in = bf(p["w_in"])
    o1 = GLA_PROJ
    o2 = GLA_PROJ + GDN_PROJ
    lora0 = 2 * GLA_QK + GLA_V
    narrow = jnp.concatenate([w_in[:, :, lora0:lora0 + GLA_LORA],
                              w_in[:, :, o1 + GDN_CONV_CH:o1 + GDN_CONV_CH + 2 * GDN_HEADS]], axis=2)
    w_small = jnp.pad(narrow, ((0, 0), (0, 0), (0, ZS_W - narrow.shape[2])))
    w_gla = jnp.concatenate([w_in[:, :, 0:lora0], w_in[:, :, lora0 + GLA_LORA:o1], w_small], axis=2)
    w_gdn = jnp.concatenate([w_in[:, :, o1:o1 + GDN_CONV_CH], w_in[:, :, o1 + GDN_CONV_CH + 2 * GDN_HEADS:o2]], axis=2)
    w_rw = w_in[:, :, o2:]
    pad_rows = lambda a, lo, hi: jnp.pad(a, ((0, 0), (lo, hi), (0, 0)))
    pad_lanes = lambda a, lo: jnp.pad(a, ((0, 0), (lo, ZS_W - lo - a.shape[1])))[:, None, :]
    row = lambda a: a.reshape(a.shape[0], 1, -1)
    return dict(
        w_gla=w_gla, w_gdn=w_gdn, w_rw=w_rw,
        w_out=bf(p["w_out"]), w_g=bf(p["w_ffn_gate"]), w_u=bf(p["w_ffn_up"]), w_d=bf(p["w_ffn_down"]),
        gla_aup=bf(pad_rows(p["gla_a_up"], 0, ZS_W - GLA_LORA)), gla_abias=row(p["gla_a_bias"]),
        gla_gn=row(p["gla_norm_g"]),
        gdn_cw=p["gdn_conv_w"], gdn_alog=pad_lanes(p["gdn_A_log"], ZS_DT), gdn_dtb=pad_lanes(p["gdn_dt_bias"], ZS_DT),
        gdn_gn=row(p["gdn_norm_g"]),
        rw_mu=row(p["rw_mu"]), rw_w0=row(p["rw_w0"]), rw_wup=bf(pad_rows(p["rw_w_up"], 0, LANES - RW_DECAY_LORA)),
        rw_a0=row(p["rw_a0"]), rw_aup=bf(pad_rows(p["rw_a_up"], RW_DECAY_LORA, 0)), rw_gup=bf(p["rw_g_up"]),
        rw_kk=row(p["rw_k_k"]), rw_ka=row(p["rw_k_a"]), rw_rk=p["rw_r_k"].reshape(DEPTH, 1, RW_C),
        rw_lng=row(p["rw_ln_g"]), rw_lnb=row(p["rw_ln_b"]),
        rw_v0=row(p["rw_v0"]), rw_vdown=bf(p["rw_v_down"]), rw_vup=bf(p["rw_v_up"]),
        norm1=row(p["norm1_g"]), norm2=row(p["norm2_g"]), final=p["final_norm_g"].reshape(1, -1),
    )


def _pair_block_diag(s):
    b, h, n, _ = s.shape
    s = s.reshape(b, h // 2, 2, n, n)
    z = jnp.zeros_like(s[:, :, 0])
    top = jnp.concatenate([s[:, :, 0], z], axis=-1)
    bot = jnp.concatenate([z, s[:, :, 1]], axis=-1)
    return jnp.concatenate([top, bot], axis=-2)


def _pair_blocks(s):
    b, hp, n2, _ = s.shape
    n = n2 // 2
    return jnp.stack([s[:, :, :n, :n], s[:, :, n:, n:]], axis=2).reshape(b, 2 * hp, n, n)


def _trunk(x, s_gla, s_gdn, c_gdn, s_rw, c_rw, w):
    b, t, d = x.shape
    m = b * t
    tm = min(ROW_TILE, m)
    tp = min(PROJ_ROW_TILE, m)
    c = min(CHUNK, t)
    nc = t // c
    xf = x.reshape(m, d)
    h = _norm_call(xf, w["norm1"][0], tm)
    outs = ([], [], [], [], [])
    vf = None
    for l in range(DEPTH):
        zg = _mm_call(h, w["w_gla"], l, tp, ZG_W + ZS_W, "proj_gla")
        zd = _mm_call(h, w["w_gdn"], l, tp, ZD_W // 2, "proj_gdn")
        zr = _mm_call(h, w["w_rw"], l, tp, RW_PROJ // 2, "proj_rw")
        zs = zg
        prm = [w[n][l] for n in ("rw_mu", "rw_w0", "rw_wup", "rw_a0", "rw_aup", "rw_gup", "rw_kk", "rw_ka",
                                 "rw_rk", "rw_lng", "rw_lnb")]
        vmix = None if l == 0 else [w[n][l - 1] for n in ("rw_v0", "rw_vdown", "rw_vup")]
        parts = dict(
            rw=_rw_part(zr, vf, vmix, prm, c_rw[l], _pair_block_diag(s_rw[l]), b, t, c, nc),
            gdn=_gdn_part(zd, zs, w["gdn_cw"][l], w["gdn_alog"][l], w["gdn_dtb"][l], w["gdn_gn"][l],
                          c_gdn[l], s_gdn[l], b, t, c, nc),
            gla=_gla_part(zg, zs, w["gla_aup"][l], w["gla_abias"][l], w["gla_gn"][l],
                          s_gla[l].reshape(b, GLA_HEADS // 2, 2 * GLA_DK, GLA_DV), b, t, c, nc))
        outs_by = dict(zip(MIXER_ORDER, _mixers_call([parts[n] for n in MIXER_ORDER], b, nc)))
        res, (od, cg, sd), (og, sg) = outs_by["rw"], outs_by["gdn"], outs_by["gla"]
        if l == 0:
            orw, vf, cr, sr = res
        else:
            orw, cr, sr = res
        flat = lambda a: a.reshape(m, a.shape[-1])
        xf, h2 = _outproj_call(flat(og), flat(od), flat(orw), w["w_out"], l, xf, w["norm2"][l], tm)
        last = l == DEPTH - 1
        g_next = w["final"] if last else w["norm1"][l + 1]
        res = _ffn_call(h2, w["w_g"], w["w_u"], w["w_d"], l, xf, g_next, tm, FFN_TF, last)
        if last:
            (y,) = res
        else:
            xf, h = res
        for lst, arr in zip(outs, (sg.reshape(b, GLA_HEADS, GLA_DK, GLA_DV), sd, cg, _pair_blocks(sr), cr)):
            lst.append(arr)
    return y.reshape(b, t, d), [jnp.stack(lst) for lst in outs]


def kernel(x_prompt, x_sample, state_gla, state_gdn, cache_gdn_conv, state_rwkv, cache_rwkv_shift, norm1_g, w_in, gla_a_up, gla_a_bias, gla_norm_g, gdn_conv_w, gdn_A_log, gdn_dt_bias, gdn_norm_g, rw_mu, rw_w0, rw_w_up, rw_a0, rw_a_up, rw_v0, rw_v_down, rw_v_up, rw_g_up, rw_k_k, rw_k_a, rw_r_k, rw_ln_g, rw_ln_b, w_out, norm2_g, w_ffn_gate, w_ffn_up, w_ffn_down, final_norm_g):
    p = dict(norm1_g=norm1_g, w_in=w_in, gla_a_up=gla_a_up, gla_a_bias=gla_a_bias, gla_norm_g=gla_norm_g,
             gdn_conv_w=gdn_conv_w, gdn_A_log=gdn_A_log, gdn_dt_bias=gdn_dt_bias, gdn_norm_g=gdn_norm_g,
             rw_mu=rw_mu, rw_w0=rw_w0, rw_w_up=rw_w_up, rw_a0=rw_a0, rw_a_up=rw_a_up, rw_v0=rw_v0,
             rw_v_down=rw_v_down, rw_v_up=rw_v_up, rw_g_up=rw_g_up, rw_k_k=rw_k_k, rw_k_a=rw_k_a,
             rw_r_k=rw_r_k, rw_ln_g=rw_ln_g, rw_ln_b=rw_ln_b, w_out=w_out, norm2_g=norm2_g,
             w_ffn_gate=w_ffn_gate, w_ffn_up=w_ffn_up, w_ffn_down=w_ffn_down, final_norm_g=final_norm_g)
    w = _prep_weights(p)
    bp = x_prompt.shape[0]
    zeros = lambda a: jnp.zeros((DEPTH, bp) + a.shape[2:], a.dtype)
    y_p, st_p = _trunk(x_prompt, zeros(state_gla), zeros(state_gdn), zeros(cache_gdn_conv), zeros(state_rwkv),
                       zeros(cache_rwkv_shift), w)
    y_s, st_s = _trunk(x_sample, state_gla, state_gdn, cache_gdn_conv, state_rwkv, cache_rwkv_shift, w)
    return (y_p, y_s, *st_p, *st_s)
```

```python
import functools

import numpy as np
import jax
import jax.numpy as jnp
from jax import lax
from jax.experimental import pallas as pl
from jax.experimental.pallas import tpu as pltpu

D_MODEL = 2048
DEPTH = 4
CHUNK = 64
NORM_EPS = 1e-6
L2_EPS = 1e-6
GLA_HEADS, GLA_DK, GLA_DV, GLA_LORA, GLA_GATE_TEMP = 4, 64, 128, 16, 16.0
GDN_HEADS, GDN_DK, GDN_DV, CONV_W = 6, 128, 128, 4
RW_HEADS, RW_N = 12, 64
RW_DECAY_LORA, RW_AAA_LORA, RW_MV_LORA, RW_GATE_LORA = 64, 64, 32, 128
RW_GN_EPS = 64e-5

GLA_QK = GLA_HEADS * GLA_DK
GLA_V = GLA_HEADS * GLA_DV
GDN_QK = GDN_HEADS * GDN_DK
GDN_V = GDN_HEADS * GDN_DV
GDN_CONV_CH = 2 * GDN_QK + GDN_V
RW_C = RW_HEADS * RW_N
GLA_PROJ = 2 * GLA_QK + GLA_V + GLA_LORA + GLA_V
GDN_PROJ = GDN_CONV_CH + 2 * GDN_HEADS + GDN_V
RW_PROJ = 3 * RW_C + RW_DECAY_LORA + RW_AAA_LORA + RW_GATE_LORA
D_FF = -(-8 * D_MODEL // (3 * 256)) * 256

LANES = 128
SUBLANES = 8
ZG_W = 2 * GLA_QK + 2 * GLA_V
ZD_W = GDN_CONV_CH + GDN_V
ZS_W = LANES
ZS_BETA = GLA_LORA
ZS_DT = GLA_LORA + GDN_HEADS
VMEM_LIMIT = 52 * 1024 * 1024
ROW_TILE = 512
PROJ_ROW_TILE = 1024
FFN_TF = 512

_BF = jnp.bfloat16
_F32 = jnp.float32
_NEG = -1e30


def _dot(a, b):
    return jnp.dot(a.astype(_BF), b.astype(_BF), preferred_element_type=_F32)


def _dot_nt(a, b):
    return lax.dot_general(a.astype(_BF), b.astype(_BF), (((1,), (1,)), ((), ())),
                           preferred_element_type=_F32)


def _dot_tn(a, b):
    return lax.dot_general(a.astype(_BF), b.astype(_BF), (((0,), (0,)), ((), ())),
                           preferred_element_type=_F32)


def _split2(x):
    hi = x.astype(_BF)
    lo = (x - hi.astype(_F32)).astype(_BF)
    return hi, lo


def _cdot(cc, x):
    return jnp.dot(cc, jnp.concatenate(_split2(x), axis=0), preferred_element_type=_F32)


def _cdot_tn(x, c):
    n = x.shape[1]
    y = lax.dot_general(jnp.concatenate(_split2(x), axis=1), c, (((0,), (0,)), ((), ())),
                        preferred_element_type=_F32)
    return y[:n] + y[n:]


def _sigmoid(x):
    return 1.0 / (1.0 + jnp.exp(-x))


def _silu(x):
    return x * _sigmoid(x)


def _softplus(x):
    return jnp.maximum(x, 0.0) + jnp.log(1.0 + jnp.exp(-jnp.abs(x)))


HALF = LANES // 2


def _rows_at(x, first, total=LANES):
    n, w = x.shape
    parts = [jnp.zeros((first, w), x.dtype)] if first else []
    parts.append(x)
    if total - first - n:
        parts.append(jnp.zeros((total - first - n, w), x.dtype))
    return jnp.concatenate(parts, axis=0) if len(parts) > 1 else x


def _rows_pair(a, b):
    return jnp.concatenate([_rows_at(a, 0, HALF), _rows_at(b, 0, HALF)], axis=0)


def _inverse_start(lows, n):
    rows = lax.broadcasted_iota(jnp.int32, (n, LANES), 0)
    lane = lax.broadcasted_iota(jnp.int32, (n, LANES), 1)
    return [jnp.where(lane == rows + HALF, 1.0, 0.0) - low for low in lows]


def _inverse_step(ws, n):
    keep = lax.broadcasted_iota(jnp.int32, (n, LANES), 1) >= HALF
    wbs = [w.astype(_BF) for w in ws]
    ys = [jnp.dot(wb, _rows_at(wb, 0), preferred_element_type=_F32) for wb in wbs]
    return [y + jnp.where(keep, w, 0.0) for y, w in zip(ys, ws)]


def _solve_with(w, rhs):
    n = rhs.shape[1]
    both = jnp.concatenate([_rows_at(part, HALF) for part in _split2(rhs)], axis=1)
    y = jnp.dot(w.astype(_BF), both, preferred_element_type=_F32)
    return y[:, :n] + y[:, n:]


def _lane_halves(shape):
    lane = lax.broadcasted_iota(jnp.int32, shape, len(shape) - 1)
    return lane < (LANES // 2)


def _half_sum(x, half):
    s0 = jnp.sum(jnp.where(half, x, 0.0), axis=-1, keepdims=True)
    s1 = jnp.sum(jnp.where(half, 0.0, x), axis=-1, keepdims=True)
    return jnp.where(half, s0, s1)


def _rmsnorm_rows(x, g):
    return x * lax.rsqrt(jnp.mean(x * x, axis=-1, keepdims=True) + NORM_EPS) * g


def _params(*sem):
    return pltpu.CompilerParams(dimension_semantics=sem, vmem_limit_bytes=VMEM_LIMIT)


def _norm_kernel(x_ref, g_ref, o_ref):
    o_ref[...] = _rmsnorm_rows(x_ref[...], g_ref[...]).astype(o_ref.dtype)


def _norm_call(x, g, tm):
    m, d = x.shape
    return pl.pallas_call(
        _norm_kernel,
        grid=(m // tm,),
        in_specs=[pl.BlockSpec((tm, d), lambda i: (i, 0)), pl.BlockSpec((1, d), lambda i: (0, 0))],
        out_specs=pl.BlockSpec((tm, d), lambda i: (i, 0)),
        out_shape=jax.ShapeDtypeStruct((m, d), _BF),
        compiler_params=_params("arbitrary"),
        name="rmsnorm",
    )(x, g)


def _mm_kernel(a_ref, w_ref, o_ref):
    o_ref[...] = jnp.dot(a_ref[...], w_ref[...], preferred_element_type=_F32)


def _mm_call(a, w, l, tm, tn, name):
    m, k = a.shape
    n = w.shape[2]
    return pl.pallas_call(
        _mm_kernel,
        grid=(n // tn, m // tm),
        in_specs=[pl.BlockSpec((tm, k), lambda j, i: (i, 0)), pl.BlockSpec((None, k, tn), lambda j, i: (l, 0, j))],
        out_specs=pl.BlockSpec((tm, tn), lambda j, i: (i, j)),
        out_shape=jax.ShapeDtypeStruct((m, n), _F32),
        compiler_params=_params("arbitrary", "arbitrary"),
        name=name,
    )(a, w)


def _outproj_kernel(og_ref, od_ref, or_ref, w_ref, x_ref, g_ref, x1_ref, h_ref):
    d = functools.partial(jnp.dot, preferred_element_type=_F32)
    acc = (d(og_ref[...], w_ref[0:GLA_V, :]) + d(od_ref[...], w_ref[GLA_V:GLA_V + GDN_V, :])
           + d(or_ref[...], w_ref[GLA_V + GDN_V:, :]))
    x1 = x_ref[...] + acc
    x1_ref[...] = x1
    h_ref[...] = _rmsnorm_rows(x1, g_ref[...]).astype(h_ref.dtype)


def _outproj_call(og, od, orw, w, l, x, g, tm):
    m, d = x.shape
    row = lambda width: pl.BlockSpec((tm, width), lambda i: (i, 0))
    return pl.pallas_call(
        _outproj_kernel,
        grid=(m // tm,),
        in_specs=[row(GLA_V), row(GDN_V), row(RW_C), pl.BlockSpec((None,) + w.shape[1:], lambda i: (l, 0, 0)), row(d),
                  pl.BlockSpec((1, d), lambda i: (0, 0))],
        out_specs=[row(d), row(d)],
        out_shape=[jax.ShapeDtypeStruct((m, d), _F32), jax.ShapeDtypeStruct((m, d), _BF)],
        compiler_params=_params("arbitrary"),
        name="outproj",
    )(og, od, orw, w, x, g)


def _ffn_kernel(h_ref, wg_ref, wu_ref, wd_ref, x_ref, g_ref, *rest, nf, emit_x):
    if emit_x:
        x2_ref, hn_ref, acc_ref = rest
    else:
        hn_ref, acc_ref = rest
    f = pl.program_id(1)

    def tile():
        h = h_ref[...]
        gate = jnp.dot(h, wg_ref[...], preferred_element_type=_F32)
        up = jnp.dot(h, wu_ref[...], preferred_element_type=_F32)
        act = (_silu(gate) * up).astype(_BF)
        return jnp.dot(act, wd_ref[...], preferred_element_type=_F32)

    @pl.when(f == 0)
    def _():
        acc_ref[...] = tile()

    @pl.when((f > 0) & (f < nf - 1))
    def _():
        acc_ref[...] += tile()

    @pl.when(f == nf - 1)
    def _():
        x2 = x_ref[...] + (acc_ref[...] + tile())
        if emit_x:
            x2_ref[...] = x2
        hn_ref[...] = _rmsnorm_rows(x2, g_ref[...]).astype(hn_ref.dtype)


def _ffn_call(h, wg, wu, wd, l, x, g, tm, tf, last):
    m, d = x.shape
    nf = D_FF // tf
    assert nf >= 2 and nf * tf == D_FF
    row = pl.BlockSpec((tm, d), lambda i, f: (i, 0))
    if last:
        out_specs = [row]
        out_shape = [jax.ShapeDtypeStruct((m, d), _F32)]
    else:
        out_specs = [row, row]
        out_shape = [jax.ShapeDtypeStruct((m, d), _F32), jax.ShapeDtypeStruct((m, d), _BF)]
    return pl.pallas_call(
        functools.partial(_ffn_kernel, nf=nf, emit_x=not last),
        grid=(m // tm, nf),
        in_specs=[row, pl.BlockSpec((None, d, tf), lambda i, f: (l, 0, f)),
                  pl.BlockSpec((None, d, tf), lambda i, f: (l, 0, f)),
                  pl.BlockSpec((None, tf, d), lambda i, f: (l, f, 0)), row, pl.BlockSpec((1, d), lambda i, f: (0, 0))],
        out_specs=out_specs,
        out_shape=out_shape,
        scratch_shapes=[pltpu.VMEM((tm, d), _F32)],
        compiler_params=_params("arbitrary", "arbitrary"),
        name="ffn",
    )(h, wg, wu, wd, x, g)


def _gla_consts(c):
    nlev = int(np.log2(c))
    t = np.arange(c)
    blocks = [t[:, None] >= t[None, :],
              t[None, :] > t[:, None]]
    mq, mk, masks = [], [], []
    for lv in range(nlev):
        m = c >> lv
        half = m // 2
        blk, pos = t // m, t % m
        mid = blk * m + half
        upper = pos >= half
        mq.append(upper[:, None] & (t[None, :] >= mid[:, None]) & (t[None, :] <= t[:, None]))
        mk.append((~upper)[:, None] & (t[None, :] > t[:, None]) & (t[None, :] <= mid[:, None] - 1))
        masks.append((blk[:, None] == blk[None, :]) & upper[:, None] & (~upper)[None, :])
    mc = np.concatenate(blocks + mq + mk, axis=0).astype(np.float32)
    return jnp.asarray(np.tile(mc, (1, 2)), _BF), jnp.asarray(np.stack(masks).astype(np.float32)), nlev


def _gla_body(zg_ref, zs_ref, aup_ref, abias_ref, gn_ref, s0_ref, mc_ref, mk_ref,
              o_ref, sout_ref, s_scr, *, c, nlev, nc):
    step = pl.program_id(1)

    @pl.when(step == 0)
    def _():
        s_scr[...] = s0_ref[...]

    xa = _dot(zs_ref[...], aup_ref[...]) + abias_ref[...]
    la = (jnp.minimum(xa, 0.0) - jnp.log(1.0 + jnp.exp(-jnp.abs(xa)))) * (1.0 / GLA_GATE_TEMP)
    yield
    ex = jnp.exp(_cdot(mc_ref[...], la))
    yield
    q = zg_ref[:, 0:GLA_QK] * (GLA_DK ** -0.5)
    k = zg_ref[:, GLA_QK:2 * GLA_QK]
    qd = q * ex[0:c]
    kd = k * ex[c:2 * c]
    half = _lane_halves((c, LANES))
    eye = lax.broadcasted_iota(jnp.int32, (c, c), 0) == lax.broadcasted_iota(jnp.int32, (c, c), 1)
    ones = jnp.ones((c, LANES), _BF)
    heads = range(GLA_HEADS)
    sls = [slice(LANES * (h // 2), LANES * (h // 2 + 1)) for h in heads]
    sel = [(lambda x: jnp.where(half, x, 0.0)) if h % 2 == 0 else (lambda x: jnp.where(half, 0.0, x)) for h in heads]
    v = [zg_ref[:, 2 * GLA_QK + GLA_DV * h:2 * GLA_QK + GLA_DV * (h + 1)] for h in heads]
    s_old = [s_scr[p] for p in range(GLA_HEADS // 2)]
    prs = range(GLA_HEADS // 2)
    psl = [sls[2 * p] for p in prs]
    both = lambda x: jnp.concatenate([sel[0](x), sel[1](x)], axis=0)
    split = lambda xs: [x[e * c:(e + 1) * c] for x in xs for e in (0, 1)]
    qk = q * k
    a = [jnp.where(eye, jnp.sum(sel[h](qk[:, sls[h]]), axis=-1, keepdims=True), 0.0) for h in heads]
    yield
    for lv in range(nlev):
        exq = ex[(2 + lv) * c:(3 + lv) * c]
        kx = k * ex[(2 + nlev + lv) * c:(3 + nlev + lv) * c]
        qx = q * exq
        lvl = split([_dot_nt(both(qx[:, psl[p]]), kx[:, psl[p]]) for p in prs])
        a = [a[h] + mk_ref[lv] * lvl[h] for h in heads]
        yield
    inter = split([_dot(both(qd[:, psl[p]]), s_old[p]) for p in prs])
    o = [inter[h] + _dot(a[h], v[h]) for h in heads]
    yield
    ms = [jnp.mean(o[h] * o[h], axis=-1, keepdims=True) for h in heads]
    for p in range(GLA_HEADS // 2):
        sl = sls[2 * p]
        s_scr[p] = (jnp.exp(_cdot_tn(la[:, sl], ones)) * s_old[p]
                    + _dot_tn(both(kd[:, sl]), jnp.concatenate([v[2 * p], v[2 * p + 1]], axis=0)))
    for _ in range(GLA_OUT_WAIT):
        yield
    for h in heads:
        g_h = zg_ref[:, 2 * GLA_QK + GLA_V + GLA_DV * h:2 * GLA_QK + GLA_V + GLA_DV * (h + 1)]
        y = o[h] * lax.rsqrt(ms[h] + NORM_EPS) * gn_ref[...] * _silu(g_h)
        o_ref[:, GLA_DV * h:GLA_DV * (h + 1)] = y.astype(o_ref.dtype)

    @pl.when(step == nc - 1)
    def _():
        sout_ref[...] = s_scr[...]


MIXER_ORDER = ("rw", "gdn", "gla")
GLA_OUT_WAIT = 4
GDN_OUT_WAIT = 2


class _Part:
    def __init__(self, body, args, in_specs, out_specs, out_shape, scratch):
        self.body, self.args, self.in_specs = body, list(args), list(in_specs)
        self.out_specs, self.out_shape = list(out_specs), list(out_shape)
        self.scratch = [pltpu.VMEM(shape, dtype) for shape, dtype in scratch]


def _full(a):
    return pl.BlockSpec(a.shape, lambda i, j: (0,) * a.ndim)


def _per_batch(a):
    return pl.BlockSpec((None,) + a.shape[1:], lambda i, j: (i,) + (0,) * (a.ndim - 1))


def _tok(c, w, col=0):
    return pl.BlockSpec((None, c, w), lambda i, j: (i, j, col))


ZS_COL = ZG_W // ZS_W


def _gla_part(zg, zs, aup, abias, gn, s0, b, t, c, nc):
    mc, masks, nlev = _gla_consts(c)
    return _Part(
        functools.partial(_gla_body, c=c, nlev=nlev, nc=nc),
        [zg.reshape(b, t, -1), zs.reshape(b, t, -1), aup, abias, gn, s0, mc, masks],
        [_tok(c, ZG_W), _tok(c, ZS_W, ZS_COL), _full(aup), _full(abias), _full(gn), _per_batch(s0), _full(mc),
         _full(masks)],
        [_tok(c, GLA_V), _per_batch(s0)],
        [jax.ShapeDtypeStruct((b, t, GLA_V), _BF), jax.ShapeDtypeStruct(s0.shape, _F32)],
        [(s0.shape[1:], _F32)])


def _mixers_kernel(*refs, bodies, counts):
    n_in, n_out = sum(x[0] for x in counts), sum(x[1] for x in counts)
    i0, o0, s0 = 0, n_in, n_in + n_out
    gens = []
    for body, (ni, no, ns) in zip(bodies, counts):
        gens.append(body(*refs[i0:i0 + ni], *refs[o0:o0 + no], *refs[s0:s0 + ns]))
        i0, o0, s0 = i0 + ni, o0 + no, s0 + ns
    while gens:
        gens = [g for g in gens if next(g, StopIteration) is not StopIteration]


def _mixers_call(parts, b, nc):
    outs = pl.pallas_call(
        functools.partial(_mixers_kernel, bodies=[p.body for p in parts],
                          counts=[(len(p.args), len(p.out_shape), len(p.scratch)) for p in parts]),
        grid=(b, nc),
        in_specs=[s for p in parts for s in p.in_specs],
        out_specs=[s for p in parts for s in p.out_specs],
        out_shape=[s for p in parts for s in p.out_shape],
        scratch_shapes=[s for p in parts for s in p.scratch],
        compiler_params=_params("arbitrary", "arbitrary"),
        name="mixers",
    )(*[a for p in parts for a in p.args])
    res, k = [], 0
    for p in parts:
        res.append(outs[k:k + len(p.out_shape)])
        k += len(p.out_shape)
    return res


def _gdn_body(zd_ref, zs_ref, cw_ref, alog_ref, dtb_ref, gn_ref, cache_ref, s0_ref, tril_ref, triu_ref, mlast_ref,
              o_ref, cout_ref, sout_ref, cbuf, s_scr, *, c, nc):
    step = pl.program_id(1)
    top = SUBLANES
    tail0 = top - (CONV_W - 1)

    @pl.when(step == 0)
    def _():
        cbuf[tail0:top, :] = cache_ref[...]
        s_scr[...] = s0_ref[...]

    x = zd_ref[:, 0:GDN_CONV_CH]
    cbuf[top:top + c, :] = x
    y = cbuf[tail0:tail0 + c, :] * cw_ref[0:1, :]
    for j in range(1, CONV_W - 1):
        y = y + cbuf[tail0 + j:tail0 + j + c, :] * cw_ref[j:j + 1, :]
    y = y + x * cw_ref[CONV_W - 1:CONV_W, :]
    tail = cbuf[c + tail0:c + top, :]
    cbuf[tail0:top, :] = tail

    @pl.when(step == nc - 1)
    def _():
        cout_ref[...] = tail

    yield
    ys = _silu(y)
    yield
    zs = zs_ref[...]
    lg = -jnp.exp(alog_ref[...]) * _softplus(zs + dtb_ref[...])
    bt = _sigmoid(zs)
    bcum = _cdot(tril_ref[...], lg)
    bcum_t = _cdot_tn(lg, triu_ref[...])
    bdl = _cdot(mlast_ref[...], lg)
    btot_t = _cdot_tn(lg, jnp.ones((c, LANES), _BF))
    lane = lax.broadcasted_iota(jnp.int32, (c, LANES), 1)
    rows = lax.broadcasted_iota(jnp.int32, (c, LANES), 0)
    causal = (rows >= lane) & (lane < c)
    strict = (rows > lane) & (lane < c)
    heads = range(GDN_HEADS)
    pick = lambda tile, j: jnp.sum(jnp.where(lane == j, tile, 0.0), axis=1, keepdims=True)
    bcol = [pick(bcum, ZS_DT + h) for h in heads]
    beta = [pick(bt, ZS_BETA + h) for h in heads]
    dlcol = [pick(bdl, ZS_DT + h) for h in heads]
    dec = [jnp.exp(jnp.where(causal, bcol[h] - bcum_t[ZS_DT + h:ZS_DT + h + 1, :], _NEG)) for h in heads]
    eb = [jnp.exp(bcol[h]) for h in heads]
    yield
    q, k, v = [], [], []
    for h in heads:
        q_h = ys[:, GDN_DK * h:GDN_DK * (h + 1)]
        k_h = ys[:, GDN_QK + GDN_DK * h:GDN_QK + GDN_DK * (h + 1)]
        q.append(q_h * lax.rsqrt(jnp.sum(q_h * q_h, axis=-1, keepdims=True) + L2_EPS) * (GDN_DK ** -0.5))
        k.append(k_h * lax.rsqrt(jnp.sum(k_h * k_h, axis=-1, keepdims=True) + L2_EPS))
        v.append(ys[:, 2 * GDN_QK + GDN_DV * h:2 * GDN_QK + GDN_DV * (h + 1)])
    qk = [jnp.concatenate([q[h], k[h]], axis=0) for h in heads]
    s_old = [s_scr[h] for h in heads]
    yield
    sc = [_dot_nt(qk[h], _rows_at(k[h], 0)) for h in heads]
    ps = [_dot(qk[h], s_old[h]) for h in heads]
    yield
    ws = _inverse_start([jnp.where(strict, beta[h] * dec[h] * sc[h][c:], 0.0) for h in heads], c)
    for _ in range(int(np.log2(c))):
        ws = _inverse_step(ws, c)
        yield
    delta = [_solve_with(ws[h], beta[h] * (v[h] - eb[h] * ps[h][c:])) for h in heads]
    yield
    o = [eb[h] * ps[h][:c] + _dot(sc[h][:c] * dec[h], _rows_at(delta[h], 0)) for h in heads]
    yield
    ms = [jnp.mean(o[h] * o[h], axis=-1, keepdims=True) for h in heads]
    for h in heads:
        elast = jnp.exp(btot_t[ZS_DT + h:ZS_DT + h + 1, :])
        s_scr[h] = elast * s_old[h] + _dot_tn(k[h] * jnp.exp(dlcol[h]), delta[h])
    for _ in range(GDN_OUT_WAIT):
        yield
    for h in heads:
        g_h = zd_ref[:, GDN_CONV_CH + GDN_DV * h:GDN_CONV_CH + GDN_DV * (h + 1)]
        y_h = o[h] * lax.rsqrt(ms[h] + NORM_EPS) * gn_ref[...] * _silu(g_h)
        o_ref[:, GDN_DV * h:GDN_DV * (h + 1)] = y_h.astype(o_ref.dtype)

    @pl.when(step == nc - 1)
    def _():
        sout_ref[...] = s_scr[...]


def _tri_consts(c):
    t = np.arange(c)
    tril = (t[:, None] >= t[None, :]).astype(np.float32)
    mlast = (t[None, :] > t[:, None]).astype(np.float32)
    triu_wide = np.pad(tril.T, ((0, 0), (0, LANES - c)))
    twice = lambda a: jnp.asarray(np.tile(a, (1, 2)), _BF)
    return twice(tril), jnp.asarray(triu_wide, _BF), twice(mlast)


def _gdn_part(zd, zs, cw, alog, dtb, gn, cache, s0, b, t, c, nc):
    tril, triu, mlast = _tri_consts(c)
    return _Part(
        functools.partial(_gdn_body, c=c, nc=nc),
        [zd.reshape(b, t, -1), zs.reshape(b, t, -1), cw, alog, dtb, gn, cache, s0, tril, triu, mlast],
        [_tok(c, ZD_W), _tok(c, ZS_W, ZS_COL), _full(cw), _full(alog), _full(dtb), _full(gn), _per_batch(cache),
         _per_batch(s0), _full(tril), _full(triu), _full(mlast)],
        [_tok(c, GDN_V), _per_batch(cache), _per_batch(s0)],
        [jax.ShapeDtypeStruct((b, t, GDN_V), _BF), jax.ShapeDtypeStruct(cache.shape, _F32),
         jax.ShapeDtypeStruct(s0.shape, _F32)],
        [((c + SUBLANES, GDN_CONV_CH), _F32), (s0.shape[1:], _F32)])


def _rw_body(*refs, c, nc, first):
    if first:
        (zr_ref, mu_ref, w0_ref, wup_ref, a0_ref, aup_ref, gup_ref, kk_ref, ka_ref, rk_ref, lng_ref, lnb_ref,
         cache_ref, s0_ref, tril_ref, mlast_ref, o_ref, vf_out_ref, shout_ref, sout_ref, sbuf, s_scr) = refs
    else:
        (zr_ref, vf_ref, v0_ref, vdown_ref, vup_ref, mu_ref, w0_ref, wup_ref, a0_ref, aup_ref, gup_ref, kk_ref,
         ka_ref, rk_ref, lng_ref, lnb_ref, cache_ref, s0_ref, tril_ref, mlast_ref, o_ref, shout_ref, sout_ref,
         sbuf, s_scr) = refs
    step = pl.program_id(1)
    top = SUBLANES

    @pl.when(step == 0)
    def _():
        sbuf[top - 1:top, :] = cache_ref[...]
        s_scr[...] = s0_ref[...]

    x = zr_ref[...]
    sbuf[top:top + c, :] = x
    zprev = sbuf[top - 1:top - 1 + c, :]
    last = sbuf[c + top - 1:c + top, :]
    sbuf[top - 1:top, :] = last

    @pl.when(step == nc - 1)
    def _():
        shout_ref[...] = last

    yield
    zm = x + (zprev - x) * mu_ref[...]
    xr, xk, xv = zm[:, 0:RW_C], zm[:, RW_C:2 * RW_C], zm[:, 2 * RW_C:3 * RW_C]
    xwa = zm[:, 3 * RW_C:3 * RW_C + LANES]
    xg = zm[:, 3 * RW_C + LANES:]
    wlog = -_softplus(-(w0_ref[...] + _dot(jnp.tanh(xwa), wup_ref[...]))) - 0.5
    lw = -jnp.exp(wlog)
    a = _sigmoid(a0_ref[...] + _dot(xwa, aup_ref[...]))
    yield
    if first:
        vf_out_ref[...] = xv
    else:
        nu = _sigmoid(v0_ref[...] + _dot(_dot(xv, vdown_ref[...]), vup_ref[...]))
        xv = xv + (vf_ref[...] - xv) * nu
    kkp = xk * kk_ref[...]
    xk2 = xk * (1.0 + (a - 1.0) * ka_ref[...])
    gate = _dot(_sigmoid(xg), gup_ref[...])
    cum = _cdot(tril_ref[...], lw)
    yield
    e_c = jnp.exp(cum)
    e_cp = jnp.exp(cum - lw)
    e_nc = jnp.exp(-cum)
    e_dl = jnp.exp(_cdot(mlast_ref[...], lw))
    half = _lane_halves((c, LANES))
    half2 = _lane_halves((2 * c, LANES))
    rows = lax.broadcasted_iota(jnp.int32, (c, LANES), 0)
    cols = lax.broadcasted_iota(jnp.int32, (c, LANES), 1) & (HALF - 1)
    strict = (rows > cols) & (cols < c)
    causal = (rows >= cols) & (cols < c)
    srow = lax.broadcasted_iota(jnp.int32, (LANES, LANES), 0) < (LANES // 2)
    scol = lax.broadcasted_iota(jnp.int32, (LANES, LANES), 1) < (LANES // 2)
    same_head = srow == scol
    ones = jnp.ones((c, LANES), _BF)
    pairs = range(RW_HEADS // 2)
    sls = [slice(LANES * p, LANES * (p + 1)) for p in pairs]
    r, k, v = [xr[:, s] for s in sls], [xk2[:, s] for s in sls], [xv[:, s] for s in sls]
    kap = [kkp[:, s] for s in sls]
    kap = [x * lax.rsqrt(_half_sum(x * x, half) + L2_EPS) for x in kap]
    ahat = [-(kap[p] * a[:, sls[p]]) for p in pairs]
    x2 = [jnp.concatenate([kap[p] * e_cp[:, sls[p]], r[p] * e_c[:, sls[p]]], axis=0) for p in pairs]
    at = [ahat[p] * e_nc[:, sls[p]] for p in pairs]
    kt = [k[p] * e_nc[:, sls[p]] for p in pairs]
    s_bd = [s_scr[p] for p in pairs]
    bsum = [_half_sum(r[p] * k[p] * rk_ref[:, sls[p]], half) for p in pairs]
    yield
    ps = [_dot(x2[p], s_bd[p]) for p in pairs]
    sc = []
    for p in pairs:
        x2b = x2[p].astype(_BF)
        both = jnp.concatenate([jnp.where(half2, x2b, 0), jnp.where(half2, 0, x2b)], axis=0)
        both = _dot_nt(both, _rows_pair(at[p], kt[p]))
        sc += [both[:2 * c], both[2 * c:]]
    yield
    ws = _inverse_start([jnp.where(half & strict, -x[:c], 0.0) for x in sc], c)
    akv = [_dot(jnp.concatenate([jnp.where(half, 0.0, jnp.where(strict, sc[2 * p + e][:c], 0.0)) for e in (0, 1)], axis=0),
                _rows_at(v[p], HALF)) for p in pairs]
    rhs = [ps[p][:c] + jnp.where(half, akv[p][:c], akv[p][c:]) for p in pairs]
    for _ in range(int(np.log2(c))):
        ws = _inverse_step(ws, c)
        yield
    u = [_solve_with(jnp.concatenate([ws[2 * p], ws[2 * p + 1]], axis=0), rhs[p]) for p in pairs]
    u = [jnp.where(half, x[:c], x[c:]) for x in u]
    yield
    oh = [_dot(jnp.concatenate([jnp.where(causal, sc[2 * p + e][c:], 0.0) for e in (0, 1)], axis=0),
               _rows_pair(u[p], v[p])) for p in pairs]
    o = [ps[p][c:] + jnp.where(half, oh[p][:c], oh[p][c:]) for p in pairs]
    yield
    mean = [_half_sum(o[p], half) * (1.0 / RW_N) for p in pairs]
    yield
    for p in pairs:
        dl = e_dl[:, sls[p]]
        upd = _dot_tn(jnp.concatenate([ahat[p] * dl, k[p] * dl], axis=0), jnp.concatenate([u[p], v[p]], axis=0))
        s_scr[p] = jnp.exp(_cdot_tn(lw[:, sls[p]], ones)) * s_bd[p] + jnp.where(same_head, upd, 0.0)
    dev = [o[p] - mean[p] for p in pairs]
    var = [_half_sum(dev[p] * dev[p], half) * (1.0 / RW_N) for p in pairs]
    yield
    for p in pairs:
        on = dev[p] * lax.rsqrt(var[p] + RW_GN_EPS) * lng_ref[:, sls[p]] + lnb_ref[:, sls[p]]
        o_ref[:, sls[p]] = ((on + bsum[p] * v[p]) * gate[:, sls[p]]).astype(o_ref.dtype)

    @pl.when(step == nc - 1)
    def _():
        sout_ref[...] = s_scr[...]


def _rw_part(zr, vf, vmix, prm, cache, s0, b, t, c, nc):
    tril, _, mlast = _tri_consts(c)
    first = vf is None
    args = ([zr.reshape(b, t, -1)] + ([] if first else [vf.reshape(b, t, -1)] + list(vmix)) + list(prm)
            + [cache, s0, tril, mlast])
    in_specs = ([_tok(c, RW_PROJ)] + ([] if first else [_tok(c, RW_C)] + [_full(a) for a in vmix])
                + [_full(a) for a in prm] + [_per_batch(cache), _per_batch(s0), _full(tril), _full(mlast)])
    out_specs = [_tok(c, RW_C)] + ([_tok(c, RW_C)] if first else []) + [_per_batch(cache), _per_batch(s0)]
    out_shape = ([jax.ShapeDtypeStruct((b, t, RW_C), _BF)] + ([jax.ShapeDtypeStruct((b, t, RW_C), _F32)] if first else [])
                 + [jax.ShapeDtypeStruct(cache.shape, _F32), jax.ShapeDtypeStruct(s0.shape, _F32)])
    return _Part(functools.partial(_rw_body, c=c, nc=nc, first=first), args, in_specs, out_specs, out_shape,
                 [((c + SUBLANES, RW_PROJ), _F32), (s0.shape[1:], _F32)])


def _prep_weights(p):
    bf = lambda a: a.astype(_BF)
    w_in = bf(p["w_in"])
    o1 = GLA_PROJ
    o2 = GLA_PROJ + GDN_PROJ
    lora0 = 2 * GLA_QK + GLA_V
    narrow = jnp.concatenate([w_in[:, :, lora0:lora0 + GLA_LORA],
                              w_in[:, :, o1 + GDN_CONV_CH:o1 + GDN_CONV_CH + 2 * GDN_HEADS]], axis=2)
    w_small = jnp.pad(narrow, ((0, 0), (0, 0), (0, ZS_W - narrow.shape[2])))
    w_gla = jnp.concatenate([w_in[:, :, 0:lora0], w_in[:, :, lora0 + GLA_LORA:o1], w_small], axis=2)
    w_gdn = jnp.concatenate([w_in[:, :, o1:o1 + GDN_CONV_CH], w_in[:, :, o1 + GDN_CONV_CH + 2 * GDN_HEADS:o2]], axis=2)
    w_rw = w_in[:, :, o2:]
    pad_rows = lambda a, lo, hi: jnp.pad(a, ((0, 0), (lo, hi), (0, 0)))
    pad_lanes = lambda a, lo: jnp.pad(a, ((0, 0), (lo, ZS_W - lo - a.shape[1])))[:, None, :]
    row = lambda a: a.reshape(a.shape[0], 1, -1)
    return dict(
        w_gla=w_gla, w_gdn=w_gdn, w_rw=w_rw,
        w_out=bf(p["w_out"]), w_g=bf(p["w_ffn_gate"]), w_u=bf(p["w_ffn_up"]), w_d=bf(p["w_ffn_down"]),
        gla_aup=bf(pad_rows(p["gla_a_up"], 0, ZS_W - GLA_LORA)), gla_abias=row(p["gla_a_bias"]),
        gla_gn=row(p["gla_norm_g"]),
        gdn_cw=p["gdn_conv_w"], gdn_alog=pad_lanes(p["gdn_A_log"], ZS_DT), gdn_dtb=pad_lanes(p["gdn_dt_bias"], ZS_DT),
        gdn_gn=row(p["gdn_norm_g"]),
        rw_mu=row(p["rw_mu"]), rw_w0=row(p["rw_w0"]), rw_wup=bf(pad_rows(p["rw_w_up"], 0, LANES - RW_DECAY_LORA)),
        rw_a0=row(p["rw_a0"]), rw_aup=bf(pad_rows(p["rw_a_up"], RW_DECAY_LORA, 0)), rw_gup=bf(p["rw_g_up"]),
        rw_kk=row(p["rw_k_k"]), rw_ka=row(p["rw_k_a"]), rw_rk=p["rw_r_k"].reshape(DEPTH, 1, RW_C),
        rw_lng=row(p["rw_ln_g"]), rw_lnb=row(p["rw_ln_b"]),
        rw_v0=row(p["rw_v0"]), rw_vdown=bf(p["rw_v_down"]), rw_vup=bf(p["rw_v_up"]),
        norm1=row(p["norm1_g"]), norm2=row(p["norm2_g"]), final=p["final_norm_g"].reshape(1, -1),
    )


def _pair_block_diag(s):
    b, h, n, _ = s.shape
    s = s.reshape(b, h // 2, 2, n, n)
    z = jnp.zeros_like(s[:, :, 0])
    top = jnp.concatenate([s[:, :, 0], z], axis=-1)
    bot = jnp.concatenate([z, s[:, :, 1]], axis=-1)
    return jnp.concatenate([top, bot], axis=-2)


def _pair_blocks(s):
    b, hp, n2, _ = s.shape
    n = n2 // 2
    return jnp.stack([s[:, :, :n, :n], s[:, :, n:, n:]], axis=2).reshape(b, 2 * hp, n, n)


def _trunk(x, s_gla, s_gdn, c_gdn, s_rw, c_rw, w):
    b, t, d = x.shape
    m = b * t
    tm = min(ROW_TILE, m)
    tp = min(PROJ_ROW_TILE, m)
    c = min(CHUNK, t)
    nc = t // c
    xf = x.reshape(m, d)
    h = _norm_call(xf, w["norm1"][0], tm)
    outs = ([], [], [], [], [])
    vf = None
    for l in range(DEPTH):
        zg = _mm_call(h, w["w_gla"], l, tp, ZG_W + ZS_W, "proj_gla")
        zd = _mm_call(h, w["w_gdn"], l, tp, ZD_W // 2, "proj_gdn")
        zr = _mm_call(h, w["w_rw"], l, tp, RW_PROJ // 2, "proj_rw")
        zs = zg
        prm = [w[n][l] for n in ("rw_mu", "rw_w0", "rw_wup", "rw_a0", "rw_aup", "rw_gup", "rw_kk", "rw_ka",
                                 "rw_rk", "rw_lng", "rw_lnb")]
        vmix = None if l == 0 else [w[n][l - 1] for n in ("rw_v0", "rw_vdown", "rw_vup")]
        parts = dict(
            rw=_rw_part(zr, vf, vmix, prm, c_rw[l], _pair_block_diag(s_rw[l]), b, t, c, nc),
            gdn=_gdn_part(zd, zs, w["gdn_cw"][l], w["gdn_alog"][l], w["gdn_dtb"][l], w["gdn_gn"][l],
                          c_gdn[l], s_gdn[l], b, t, c, nc),
            gla=_gla_part(zg, zs, w["gla_aup"][l], w["gla_abias"][l], w["gla_gn"][l],
                          s_gla[l].reshape(b, GLA_HEADS // 2, 2 * GLA_DK, GLA_DV), b, t, c, nc))
        outs_by = dict(zip(MIXER_ORDER, _mixers_call([parts[n] for n in MIXER_ORDER], b, nc)))
        res, (od, cg, sd), (og, sg) = outs_by["rw"], outs_by["gdn"], outs_by["gla"]
        if l == 0:
            orw, vf, cr, sr = res
        else:
            orw, cr, sr = res
        flat = lambda a: a.reshape(m, a.shape[-1])
        xf, h2 = _outproj_call(flat(og), flat(od), flat(orw), w["w_out"], l, xf, w["norm2"][l], tm)
        last = l == DEPTH - 1
        g_next = w["final"] if last else w["norm1"][l + 1]
        res = _ffn_call(h2, w["w_g"], w["w_u"], w["w_d"], l, xf, g_next, tm, FFN_TF, last)
        if last:
            (y,) = res
        else:
            xf, h = res
        for lst, arr in zip(outs, (sg.reshape(b, GLA_HEADS, GLA_DK, GLA_DV), sd, cg, _pair_blocks(sr), cr)):
            lst.append(arr)
    return y.reshape(b, t, d), [jnp.stack(lst) for lst in outs]


def kernel(x_prompt, x_sample, state_gla, state_gdn, cache_gdn_conv, state_rwkv, cache_rwkv_shift, norm1_g, w_in, gla_a_up, gla_a_bias, gla_norm_g, gdn_conv_w, gdn_A_log, gdn_dt_bias, gdn_norm_g, rw_mu, rw_w0, rw_w_up, rw_a0, rw_a_up, rw_v0, rw_v_down, rw_v_up, rw_g_up, rw_k_k, rw_k_a, rw_r_k, rw_ln_g, rw_ln_b, w_out, norm2_g, w_ffn_gate, w_ffn_up, w_ffn_down, final_norm_g):
    p = dict(norm1_g=norm1_g, w_in=w_in, gla_a_up=gla_a_up, gla_a_bias=gla_a_bias, gla_norm_g=gla_norm_g,
             gdn_conv_w=gdn_conv_w, gdn_A_log=gdn_A_log, gdn_dt_bias=gdn_dt_bias, gdn_norm_g=gdn_norm_g,
             rw_mu=rw_mu, rw_w0=rw_w0, rw_w_up=rw_w_up, rw_a0=rw_a0, rw_a_up=rw_a_up, rw_v0=rw_v0,
             rw_v_down=rw_v_down, rw_v_up=rw_v_up, rw_g_up=rw_g_up, rw_k_k=rw_k_k, rw_k_a=rw_k_a,
             rw_r_k=rw_r_k, rw_ln_g=rw_ln_g, rw_ln_b=rw_ln_b, w_out=w_out, norm2_g=norm2_g,
             w_ffn_gate=w_ffn_gate, w_ffn_up=w_ffn_up, w_ffn_down=w_ffn_down, final_norm_g=final_norm_g)
    w = _prep_weights(p)
    bp = x_prompt.shape[0]
    zeros = lambda a: jnp.zeros((DEPTH, bp) + a.shape[2:], a.dtype)
    y_p, st_p = _trunk(x_prompt, zeros(state_gla), zeros(state_gdn), zeros(cache_gdn_conv), zeros(state_rwkv),
                       zeros(cache_rwkv_shift), w)
    y_s, st_s = _trunk(x_sample, state_gla, state_gdn, cache_gdn_conv, state_rwkv, cache_rwkv_shift, w)
    return (y_p, y_s, *st_p, *st_s)
```
